```python
import jax, jax.numpy as jnp
from jax import lax
import numpy as np

D_MODEL = 1024
BATCH = 2
SEQ = 8192
DEPTH = 2

N_HEADS = 8
HEAD_DIM = 128
ATTN_WIDTH = N_HEADS * HEAD_DIM
ROT_DIM = HEAD_DIM // 4
ROPE_THETA = 500000.0
MOBA_BLOCK = 256
MOBA_TOPK = 3
Q_CHUNK = 32
CONV_WIDTH = D_MODEL
CONV_KERNEL = 31
N_BRANCH = 2
SPLIT_SIZES = (ATTN_WIDTH, ATTN_WIDTH, ATTN_WIDTH, ATTN_WIDTH,
               2 * CONV_WIDTH, CONV_WIDTH, N_BRANCH * D_MODEL)
N_IN = sum(SPLIT_SIZES)
SPLIT_POINTS = tuple(int(v) for v in np.cumsum(SPLIT_SIZES)[:-1])
EPS = 1e-6

kernel_name = "moba_conformer_gated_hybrid"


def rms_norm(t, g):
    tf = t.astype(jnp.float32)
    y = tf * lax.rsqrt(jnp.mean(tf * tf, axis=-1, keepdims=True) + EPS)
    return (y * g.astype(jnp.float32)).astype(t.dtype)


def layer_norm(t, g, b):
    tf = t.astype(jnp.float32)
    mu = jnp.mean(tf, axis=-1, keepdims=True)
    var = jnp.mean(jnp.square(tf - mu), axis=-1, keepdims=True)
    y = (tf - mu) * lax.rsqrt(var + EPS)
    return (y * g.astype(jnp.float32) + b.astype(jnp.float32)).astype(t.dtype)


def rope_tables(seq):
    inv_freq = ROPE_THETA ** (-jnp.arange(0, ROT_DIM, 2, dtype=jnp.float32) / ROT_DIM)
    ang = jnp.arange(seq, dtype=jnp.float32)[:, None] * inv_freq[None, :]
    return jnp.cos(ang), jnp.sin(ang)


def partial_rope(t, cos, sin):
    tf = t.astype(jnp.float32)
    half = ROT_DIM // 2
    t1, t2, rest = tf[..., :half], tf[..., half:ROT_DIM], tf[..., ROT_DIM:]
    out = jnp.concatenate([t1 * cos - t2 * sin, t2 * cos + t1 * sin, rest], axis=-1)
    return out.astype(t.dtype)


def moba_attention(q, k, v):
    B, H, S, D = q.shape
    nb = -(-S // MOBA_BLOCK)
    pad = nb * MOBA_BLOCK - S
    kp = jnp.pad(k, ((0, 0), (0, 0), (0, pad), (0, 0)))
    vp = jnp.pad(v, ((0, 0), (0, 0), (0, pad), (0, 0)))
    kb = kp.reshape(B, H, nb, MOBA_BLOCK, D)
    vb = vp.reshape(B, H, nb, MOBA_BLOCK, D)
    kmean = jnp.mean(kb.astype(jnp.float32), axis=3)
    topk = min(MOBA_TOPK, nb)
    n_chunks = S // Q_CHUNK
    scale = 1.0 / float(np.sqrt(D))
    neg = jnp.finfo(jnp.float32).min
    bi = jnp.arange(B)[:, None, None, None]
    hi = jnp.arange(H)[None, :, None, None]
    qc = q.reshape(B, H, n_chunks, Q_CHUNK, D).transpose(2, 0, 1, 3, 4)

    def chunk(args):
        qi, c = args
        start = c * Q_CHUNK
        blk = start // MOBA_BLOCK
        qpos = start + jnp.arange(Q_CHUNK)
        qf = qi.astype(jnp.float32)
        gate = jnp.einsum('bhqd,bhnd->bhqn', qf, kmean)
        gate = jnp.where(jnp.arange(nb) < blk, gate, neg)
        _, gidx = lax.top_k(gate, topk)
        valid = gidx < blk
        ksel = kb[bi, hi, gidx].astype(jnp.float32)
        vsel = vb[bi, hi, gidx].astype(jnp.float32)
        kown = lax.dynamic_index_in_dim(kb, blk, axis=2, keepdims=False).astype(jnp.float32)
        vown = lax.dynamic_index_in_dim(vb, blk, axis=2, keepdims=False).astype(jnp.float32)
        qs = qf * scale
        l_sel = jnp.einsum('bhqd,bhqtkd->bhqtk', qs, ksel)
        l_sel = jnp.where(valid[..., None], l_sel, neg).reshape(B, H, Q_CHUNK, topk * MOBA_BLOCK)
        kpos = blk * MOBA_BLOCK + jnp.arange(MOBA_BLOCK)
        l_own = jnp.einsum('bhqd,bhkd->bhqk', qs, kown)
        l_own = jnp.where(kpos[None, :] <= qpos[:, None], l_own, neg)
        p = jax.nn.softmax(jnp.concatenate([l_sel, l_own], axis=-1), axis=-1)
        p_sel = p[..., :topk * MOBA_BLOCK].reshape(B, H, Q_CHUNK, topk, MOBA_BLOCK)
        p_own = p[..., topk * MOBA_BLOCK:]
        o = jnp.einsum('bhqtk,bhqtkd->bhqd', p_sel, vsel) + jnp.einsum('bhqk,bhkd->bhqd', p_own, vown)
        return o.astype(q.dtype)

    out = lax.map(chunk, (qc, jnp.arange(n_chunks)))
    return out.transpose(1, 2, 0, 3, 4).reshape(B, H, S, D)


def causal_depthwise_conv(t, w, b):
    C = t.shape[-1]
    y = lax.conv_general_dilated(
        t, w.reshape(CONV_KERNEL, 1, C).astype(t.dtype),
        window_strides=(1,), padding=[(CONV_KERNEL - 1, 0)],
        dimension_numbers=('NWC', 'WIO', 'NWC'), feature_group_count=C)
    return y + b.astype(t.dtype)


def hybrid_layer(x, norm_g, w_in, b_gate, q_norm_g, k_norm_g, conv_w, conv_b,
                 cn_g, cn_b, w_attn_proj, w_conv_proj, w_out, cos, sin):
    B, S, _ = x.shape
    h = rms_norm(x, norm_g)
    proj = h @ w_in
    q, k, v, z_a, u, z_c, gl = jnp.split(proj, SPLIT_POINTS, axis=-1)

    def heads(t):
        return t.reshape(B, S, N_HEADS, HEAD_DIM).transpose(0, 2, 1, 3)

    qh = partial_rope(rms_norm(heads(q), q_norm_g), cos, sin)
    kh = partial_rope(rms_norm(heads(k), k_norm_g), cos, sin)
    o = moba_attention(qh, kh, heads(v)).transpose(0, 2, 1, 3).reshape(B, S, ATTN_WIDTH)
    y_a = (o * jax.nn.silu(z_a)) @ w_attn_proj

    ua, ub = jnp.split(u, 2, axis=-1)
    c = causal_depthwise_conv(ua * jax.nn.sigmoid(ub), conv_w, conv_b)
    c = jax.nn.silu(layer_norm(c, cn_g, cn_b))
    y_c = (c * jax.nn.silu(z_c)) @ w_conv_proj

    g_a, g_c = jnp.split(jax.nn.sigmoid(gl + b_gate), 2, axis=-1)
    return x + (g_a * y_a + g_c * y_c) @ w_out


def setup_inputs(seed: int = 0) -> dict:
    key = jax.random.key(seed)
    ks = jax.random.split(key, 13)
    L = DEPTH
    f32 = jnp.float32
    nrm = lambda k, shape, s: jax.random.normal(k, shape, f32) * s
    return {
        "x": jax.random.normal(ks[0], (BATCH, SEQ, D_MODEL), f32),
        "norm_g": 1.0 + nrm(ks[1], (L, D_MODEL), 0.02),
        "w_in": nrm(ks[2], (L, D_MODEL, N_IN), D_MODEL ** -0.5),
        "b_gate": nrm(ks[3], (L, N_BRANCH * D_MODEL), 0.02),
        "q_norm_g": 1.0 + nrm(ks[4], (L, HEAD_DIM), 0.02),
        "k_norm_g": 1.0 + nrm(ks[5], (L, HEAD_DIM), 0.02),
        "conv_w": nrm(ks[6], (L, CONV_KERNEL, CONV_WIDTH), CONV_KERNEL ** -0.5),
        "conv_b": nrm(ks[7], (L, CONV_WIDTH), 0.02),
        "cn_g": 1.0 + nrm(ks[8], (L, CONV_WIDTH), 0.02),
        "cn_b": nrm(ks[9], (L, CONV_WIDTH), 0.02),
        "w_attn_proj": nrm(ks[10], (L, ATTN_WIDTH, D_MODEL), ATTN_WIDTH ** -0.5),
        "w_conv_proj": nrm(ks[11], (L, CONV_WIDTH, D_MODEL), CONV_WIDTH ** -0.5),
        "w_out": nrm(ks[12], (L, D_MODEL, D_MODEL), D_MODEL ** -0.5),
    }


def reference(x, norm_g, w_in, b_gate, q_norm_g, k_norm_g, conv_w, conv_b,
              cn_g, cn_b, w_attn_proj, w_conv_proj, w_out):
    cos, sin = rope_tables(x.shape[1])
    for l in range(DEPTH):
        x = hybrid_layer(x, norm_g[l], w_in[l], b_gate[l], q_norm_g[l], k_norm_g[l],
                         conv_w[l], conv_b[l], cn_g[l], cn_b[l], w_attn_proj[l],
                         w_conv_proj[l], w_out[l], cos, sin)
    return x
```

```python
import functools

import jax
import jax.numpy as jnp
from jax import lax
from jax.experimental import pallas as pl
from jax.experimental.pallas import tpu as pltpu

D_MODEL = 1024
N_HEADS = 8
HEAD_DIM = 128
ROT_HALF = 16
ROPE_THETA = 500000.0
MOBA_BLOCK = 256
MOBA_TOPK = 3
CONV_KERNEL = 31
CONV_HALO = 32
EPS = 1e-6
MASKED = -1e30

IN_TILE_M = 1024
TAIL_TILE_M = 512
SUBLANES = 8
CONV_ROWS = 64
CONV_LANES = 256
NORM_ROWS = 16
VMEM_LIMIT = 56 * 1024 * 1024

P_Q, P_K, P_SZA, P_GLU, P_SZC, P_GA, P_GC = range(7)
N_P_GROUPS = 7
W_Q, W_K, W_V, W_ZA, W_UA, W_UB, W_ZC, W_GLA, W_GLC = range(9)
N_W_GROUPS = 9

_NT = (((1,), (1,)), ((), ()))


def _head_norm_rope(t, g, cos, sna, snb):
    ms = jnp.mean(t * t, axis=-1, keepdims=True)
    tn = t * lax.rsqrt(ms + EPS) * g
    return (tn * cos + pltpu.roll(tn, HEAD_DIM - ROT_HALF, 1) * sna
            + pltpu.roll(tn, ROT_HALF, 1) * snb)


def _in_proj_kernel(x_ref, ng_ref, w_ref, bg_ref, qg_ref, kg_ref, cos_ref, sna_ref, snb_ref,
                    p_ref, kmean_ref, vt_ref, h_ref, ua_ref):
    j = pl.program_id(1)
    tm = x_ref.shape[0]

    @pl.when(j == 0)
    def _():
        x = x_ref[...]
        ms = jnp.mean(x * x, axis=-1, keepdims=True)
        h_ref[...] = (x * lax.rsqrt(ms + EPS) * ng_ref[...]).astype(jnp.bfloat16)

    acc = jnp.dot(h_ref[...], w_ref[...], preferred_element_type=jnp.float32)

    @pl.when(j == W_Q)
    def _():
        scale = 1.0 / (HEAD_DIM ** 0.5)
        for h in range(N_HEADS):
            sl = slice(h * HEAD_DIM, (h + 1) * HEAD_DIM)
            t = _head_norm_rope(acc[:, sl], qg_ref[...], cos_ref[...], sna_ref[...], snb_ref[...])
            p_ref[:, sl] = (t * scale).astype(jnp.bfloat16)

    @pl.when(j == W_K)
    def _():
        for h in range(N_HEADS):
            sl = slice(h * HEAD_DIM, (h + 1) * HEAD_DIM)
            t = _head_norm_rope(acc[:, sl], kg_ref[...], cos_ref[...], sna_ref[...], snb_ref[...])
            p_ref[:, sl] = t.astype(jnp.bfloat16)
            for s in range(tm // MOBA_BLOCK):
                blk = t[s * MOBA_BLOCK:(s + 1) * MOBA_BLOCK, :]
                kmean_ref[0, s:s + 1, sl] = jnp.mean(blk, axis=0, keepdims=True)

    @pl.when(j == W_V)
    def _():
        for s in range(tm // MOBA_BLOCK):
            blk = acc[s * MOBA_BLOCK:(s + 1) * MOBA_BLOCK, :]
            vt_ref[s] = blk.T.astype(jnp.bfloat16)

    @pl.when((j == W_ZA) | (j == W_ZC))
    def _():
        p_ref[...] = (acc * jax.nn.sigmoid(acc)).astype(jnp.bfloat16)

    @pl.when(j == W_UA)
    def _():
        ua_ref[...] = acc

    @pl.when(j == W_UB)
    def _():
        p_ref[...] = (ua_ref[...] * jax.nn.sigmoid(acc)).astype(jnp.bfloat16)

    @pl.when(j >= W_GLA)
    def _():
        p_ref[...] = jax.nn.sigmoid(acc + bg_ref[...]).astype(jnp.bfloat16)


def _in_proj(x, norm_g, w_in, b_gate, q_g, k_g, cos, sna, snb, seq):
    m = x.shape[0]
    tm = IN_TILE_M
    n_seq_tiles = seq // tm
    blocks_per_tile = tm // MOBA_BLOCK

    def p_col(i, j):
        return (i, j - (j >= W_V).astype(jnp.int32) - (j >= W_UB).astype(jnp.int32))

    return pl.pallas_call(
        _in_proj_kernel,
        grid=(m // tm, N_W_GROUPS),
        in_specs=[
            pl.BlockSpec((tm, D_MODEL), lambda i, j: (i, 0)),
            pl.BlockSpec((1, D_MODEL), lambda i, j: (0, 0)),
            pl.BlockSpec((D_MODEL, D_MODEL), lambda i, j: (0, j)),
            pl.BlockSpec((1, D_MODEL), lambda i, j: (0, jnp.clip(j - W_GLA, 0, 1))),
            pl.BlockSpec((1, HEAD_DIM), lambda i, j: (0, 0)),
            pl.BlockSpec((1, HEAD_DIM), lambda i, j: (0, 0)),
            pl.BlockSpec((tm, HEAD_DIM), lambda i, j: (i % n_seq_tiles, 0)),
            pl.BlockSpec((tm, HEAD_DIM), lambda i, j: (i % n_seq_tiles, 0)),
            pl.BlockSpec((tm, HEAD_DIM), lambda i, j: (i % n_seq_tiles, 0)),
        ],
        out_specs=[
            pl.BlockSpec((tm, D_MODEL), p_col),
            pl.BlockSpec((1, blocks_per_tile, D_MODEL), lambda i, j: (i, 0, 0)),
            pl.BlockSpec((blocks_per_tile, D_MODEL, MOBA_BLOCK), lambda i, j: (i, 0, 0)),
        ],
        out_shape=[
            jax.ShapeDtypeStruct((m, N_P_GROUPS * D_MODEL), jnp.bfloat16),
            jax.ShapeDtypeStruct((m // tm, blocks_per_tile, D_MODEL), jnp.float32),
            jax.ShapeDtypeStruct((m // MOBA_BLOCK, D_MODEL, MOBA_BLOCK), jnp.bfloat16),
        ],
        scratch_shapes=[
            pltpu.VMEM((tm, D_MODEL), jnp.bfloat16),
            pltpu.VMEM((tm, D_MODEL), jnp.float32),
        ],
        compiler_params=pltpu.CompilerParams(
            dimension_semantics=("arbitrary", "arbitrary"),
            vmem_limit_bytes=VMEM_LIMIT),
        name="in_proj",
    )(x, norm_g, w_in, b_gate, q_g, k_g, cos, sna, snb)


def _attn_kernel(q_ref, k_ref, vt_ref, km_ref, o_ref, bias_ref):
    qi = pl.program_id(2)
    n_blocks = km_ref.shape[1]
    tq = q_ref.shape[0]
    q = q_ref[...]

    gate = lax.dot_general(km_ref[0], q.astype(jnp.float32), _NT,
                           preferred_element_type=jnp.float32)
    blk_iota = lax.broadcasted_iota(jnp.int32, (n_blocks, tq), 0)
    past = blk_iota < qi
    g = jnp.where(past, gate, -jnp.inf)
    chosen = jnp.zeros((n_blocks, tq), jnp.float32)
    for _ in range(MOBA_TOPK):
        top = jnp.max(g, axis=0, keepdims=True)
        first = jnp.min(jnp.where(g == top, blk_iota, n_blocks), axis=0, keepdims=True)
        pick = blk_iota == first
        chosen = jnp.where(pick, 1.0, chosen)
        g = jnp.where(pick, -jnp.inf, g)
    bias_ref[...] = jnp.where(past, jnp.where(chosen > 0.0, 0.0, MASKED), MASKED)

    own = pl.multiple_of(qi * MOBA_BLOCK, MOBA_BLOCK)
    s_t = lax.dot_general(k_ref[pl.ds(own, MOBA_BLOCK), :], q, _NT,
                          preferred_element_type=jnp.float32)
    key_i = lax.broadcasted_iota(jnp.int32, (MOBA_BLOCK, tq), 0)
    qry_i = lax.broadcasted_iota(jnp.int32, (MOBA_BLOCK, tq), 1)
    s_t = jnp.where(key_i <= qry_i, s_t, MASKED)
    m0 = jnp.max(s_t, axis=0, keepdims=True)
    p = jnp.exp(s_t - m0)
    l0 = jnp.sum(p, axis=0, keepdims=True)
    acc0 = jnp.dot(vt_ref[qi], p.astype(jnp.bfloat16), preferred_element_type=jnp.float32)

    def body(j, carry):
        m_i, l_i, acc = carry
        start = pl.multiple_of(j * MOBA_BLOCK, MOBA_BLOCK)
        s_t = lax.dot_general(k_ref[pl.ds(start, MOBA_BLOCK), :], q, _NT,
                              preferred_element_type=jnp.float32)
        s_t = s_t + bias_ref[pl.ds(j, 1), :]
        m_new = jnp.maximum(m_i, jnp.max(s_t, axis=0, keepdims=True))
        alpha = jnp.exp(m_i - m_new)
        p = jnp.exp(s_t - m_new)
        l_new = alpha * l_i + jnp.sum(p, axis=0, keepdims=True)
        acc = alpha * acc + jnp.dot(vt_ref[j], p.astype(jnp.bfloat16),
                                    preferred_element_type=jnp.float32)
        return m_new, l_new, acc

    _, l_f, acc = lax.fori_loop(0, qi, body, (m0, l0, acc0))
    o_ref[...] = (acc / l_f).T.astype(o_ref.dtype)


def _attention(p_arr, vt, kmean, batch, seq):
    m = batch * seq
    n_blocks = seq // MOBA_BLOCK
    tq = MOBA_BLOCK
    return pl.pallas_call(
        _attn_kernel,
        grid=(batch, N_HEADS, n_blocks),
        in_specs=[
            pl.BlockSpec((tq, HEAD_DIM), lambda b, h, i: (b * n_blocks + i, P_Q * N_HEADS + h)),
            pl.BlockSpec((seq, HEAD_DIM), lambda b, h, i: (b, P_K * N_HEADS + h)),
            pl.BlockSpec((n_blocks, HEAD_DIM, MOBA_BLOCK), lambda b, h, i: (b, h, 0)),
            pl.BlockSpec((1, n_blocks, HEAD_DIM), lambda b, h, i: (b, 0, h)),
        ],
        out_specs=pl.BlockSpec((tq, HEAD_DIM), lambda b, h, i: (b * n_blocks + i, h)),
        out_shape=jax.ShapeDtypeStruct((m, N_HEADS * HEAD_DIM), jnp.bfloat16),
        scratch_shapes=[pltpu.VMEM((n_blocks, tq), jnp.float32)],
        compiler_params=pltpu.CompilerParams(
            dimension_semantics=("arbitrary", "arbitrary", "arbitrary"),
            vmem_limit_bytes=VMEM_LIMIT),
        name="moba_attn",
    )(p_arr, p_arr, vt, kmean)


def _tail_kernel(glu_ref, szc_ref, o_ref, sza_ref, ga_ref, gc_ref, x_ref,
                 wa_ref, wc_ref, wo_ref, cw_ref, cb_ref, cng_ref, cnb_ref,
                 out_ref, hist_ref, conv_ref, c_ref):
    s = pl.program_id(1)
    ts = x_ref.shape[0]

    @pl.when(s == 0)
    def _():
        hist_ref[0:CONV_HALO, :] = jnp.zeros((CONV_HALO, D_MODEL), jnp.float32)

    @pl.when(s > 0)
    def _():
        hist_ref[0:CONV_HALO, :] = hist_ref[ts:ts + CONV_HALO, :]

    hist_ref[CONV_HALO:, :] = glu_ref[...].astype(jnp.float32)

    first_tap = CONV_HALO - (CONV_KERNEL - 1)
    win_rows = CONV_ROWS + CONV_HALO

    def conv_chunk(c, carry):
        r0 = pl.multiple_of(c * CONV_ROWS, CONV_ROWS)
        for lane0 in range(0, D_MODEL, CONV_LANES):
            lanes = slice(lane0, lane0 + CONV_LANES)
            win = hist_ref[pl.ds(r0, win_rows), lanes]
            acc = jnp.zeros((CONV_ROWS, CONV_LANES), jnp.float32)
            for sh in range(SUBLANES):
                shifted = win if sh == 0 else pltpu.roll(win, win_rows - sh, 0)
                for a in range(CONV_HALO // SUBLANES + 1):
                    t = SUBLANES * a + sh - first_tap
                    if 0 <= t < CONV_KERNEL:
                        acc = acc + (shifted[SUBLANES * a:SUBLANES * a + CONV_ROWS, :]
                                     * cw_ref[t:t + 1, lanes])
            conv_ref[pl.ds(r0, CONV_ROWS), lanes] = acc
        return carry

    lax.fori_loop(0, ts // CONV_ROWS, conv_chunk, 0)

    def chunk(c, carry):
        r0 = pl.multiple_of(c * NORM_ROWS, NORM_ROWS)
        acc = conv_ref[pl.ds(r0, NORM_ROWS), :] + cb_ref[...]
        mu = jnp.mean(acc, axis=-1, keepdims=True)
        d = acc - mu
        var = jnp.mean(d * d, axis=-1, keepdims=True)
        y = d * lax.rsqrt(var + EPS) * cng_ref[...] + cnb_ref[...]
        act = y * jax.nn.sigmoid(y)
        gate = szc_ref[pl.ds(r0, NORM_ROWS), :].astype(jnp.float32)
        c_ref[pl.ds(r0, NORM_ROWS), :] = (act * gate).astype(jnp.bfloat16)
        return carry

    lax.fori_loop(0, ts // NORM_ROWS, chunk, 0)

    y_c = jnp.dot(c_ref[...], wc_ref[...], preferred_element_type=jnp.float32)
    a_in = (o_ref[...].astype(jnp.float32) * sza_ref[...].astype(jnp.float32)).astype(jnp.bfloat16)
    y_a = jnp.dot(a_in, wa_ref[...], preferred_element_type=jnp.float32)
    merged = (ga_ref[...].astype(jnp.float32) * y_a
              + gc_ref[...].astype(jnp.float32) * y_c).astype(jnp.bfloat16)
    out_ref[...] = x_ref[...] + jnp.dot(merged, wo_ref[...], preferred_element_type=jnp.float32)


def _tail(p_arr, o, x, wa, wc, wo, conv_w, conv_b, cn_g, cn_b, batch, seq):
    m = batch * seq
    ts = TAIL_TILE_M
    n_s = seq // ts

    def p_block(group):
        return pl.BlockSpec((ts, D_MODEL), lambda b, s: (b * n_s + s, group))

    row_block = pl.BlockSpec((ts, D_MODEL), lambda b, s: (b * n_s + s, 0))
    full = lambda shape: pl.BlockSpec(shape, lambda b, s: (0, 0))
    return pl.pallas_call(
        _tail_kernel,
        grid=(batch, n_s),
        in_specs=[
            p_block(P_GLU), p_block(P_SZC), row_block, p_block(P_SZA), p_block(P_GA),
            p_block(P_GC), row_block,
            full((D_MODEL, D_MODEL)), full((D_MODEL, D_MODEL)), full((D_MODEL, D_MODEL)),
            full((CONV_KERNEL, D_MODEL)), full((1, D_MODEL)), full((1, D_MODEL)),
            full((1, D_MODEL)),
        ],
        out_specs=row_block,
        out_shape=jax.ShapeDtypeStruct((m, D_MODEL), jnp.float32),
        scratch_shapes=[
            pltpu.VMEM((CONV_HALO + ts, D_MODEL), jnp.float32),
            pltpu.VMEM((ts, D_MODEL), jnp.float32),
            pltpu.VMEM((ts, D_MODEL), jnp.bfloat16),
        ],
        compiler_params=pltpu.CompilerParams(
            dimension_semantics=("arbitrary", "arbitrary"),
            vmem_limit_bytes=VMEM_LIMIT),
        name="tail",
    )(p_arr, p_arr, o, p_arr, p_arr, p_arr, x, wa, wc, wo, conv_w, conv_b, cn_g, cn_b)


def _rope_tables(seq):
    inv_freq = ROPE_THETA ** (-jnp.arange(0, 2 * ROT_HALF, 2, dtype=jnp.float32) / (2 * ROT_HALF))
    ang = jnp.arange(seq, dtype=jnp.float32)[:, None] * inv_freq[None, :]
    cos, sin = jnp.cos(ang), jnp.sin(ang)
    rest = HEAD_DIM - 2 * ROT_HALF
    zeros = lambda w: jnp.zeros((seq, w), jnp.float32)
    cos_t = jnp.concatenate([cos, cos, jnp.ones((seq, rest), jnp.float32)], axis=1)
    sna_t = jnp.concatenate([-sin, zeros(HEAD_DIM - ROT_HALF)], axis=1)
    snb_t = jnp.concatenate([zeros(ROT_HALF), sin, zeros(rest)], axis=1)
    return cos_t, sna_t, snb_t


def kernel(x, norm_g, w_in, b_gate, q_norm_g, k_norm_g, conv_w, conv_b, cn_g, cn_b,
           w_attn_proj, w_conv_proj, w_out):
    batch, seq, d = x.shape
    depth = w_in.shape[0]
    assert d == D_MODEL and seq % IN_TILE_M == 0 and seq % TAIL_TILE_M == 0
    cos_t, sna_t, snb_t = _rope_tables(seq)
    bf = jnp.bfloat16
    xf = x.reshape(batch * seq, d)
    for l in range(depth):
        p_arr, kmean, vt = _in_proj(
            xf, norm_g[l][None], w_in[l].astype(bf), b_gate[l][None],
            q_norm_g[l][None], k_norm_g[l][None], cos_t, sna_t, snb_t, seq)
        kmean = kmean.reshape(batch, seq // MOBA_BLOCK, D_MODEL)
        o = _attention(p_arr, vt, kmean, batch, seq)
        xf = _tail(p_arr, o, xf, w_attn_proj[l].astype(bf), w_conv_proj[l].astype(bf),
                   w_out[l].astype(bf), conv_w[l], conv_b[l][None], cn_g[l][None],
                   cn_b[l][None], batch, seq)
    return xf.reshape(batch, seq, d)
```

```python
import jax
import jax.numpy as jnp
from jax import lax
from jax.experimental import pallas as pl
from jax.experimental.pallas import tpu as pltpu

D_MODEL = 1024
N_HEADS = 8
HEAD_DIM = 128
ROT_HALF = 16
ROPE_THETA = 500000.0
MOBA_BLOCK = 256
MOBA_TOPK = 3
CONV_KERNEL = 31
CONV_HALO = 32
EPS = 1e-6
MASKED = -1e30
LOG2_E = 1.4426950408889634
ACC_ROWS = HEAD_DIM + 16

LANES = 128
IN_TILE_M = 1024
IN_CHUNK_M = MOBA_BLOCK
TAIL_TILE_M = 512
CONV_ROWS = 32
VMEM_LIMIT = 56 * 1024 * 1024

P_Q, P_K, P_SZA, P_GLU, P_SZC, P_GA, P_GC = range(7)
N_P_GROUPS = 7
W_Q, W_K, W_V, W_ZA, W_UA, W_UB, W_ZC, W_GLA, W_GLC = range(9)
N_W_GROUPS = 9

_NT = (((1,), (1,)), ((), ()))

_HALF = HEAD_DIM // 2
_HEAD_PERM_SEGMENTS = ((0, ROT_HALF), (2 * ROT_HALF, _HALF + ROT_HALF),
                       (ROT_HALF, 2 * ROT_HALF), (_HALF + ROT_HALF, HEAD_DIM))


def _permute_head_dims(t):
    return jnp.concatenate([t[..., a:b] for a, b in _HEAD_PERM_SEGMENTS], axis=-1)


def _head_norm_rope(t, g, cos, sin):
    ms = jnp.mean(t * t, axis=-1, keepdims=True)
    tn = t * lax.rsqrt(ms + EPS) * g
    return tn * cos + pltpu.roll(tn, _HALF, 1) * sin


def _in_proj_kernel(x_ref, ng_ref, w_ref, bg_ref, qg_ref, kg_ref, cos_ref, sin_ref,
                    p_ref, kmean_ref, vt_ref, h_ref, ua_ref):
    j = pl.program_id(1)
    tm = x_ref.shape[0]
    head_lanes = [slice(h * HEAD_DIM, (h + 1) * HEAD_DIM) for h in range(N_HEADS)]

    @pl.when(j == 0)
    def _():
        x = x_ref[...]
        ms = jnp.mean(x * x, axis=-1, keepdims=True)
        h_ref[...] = (x * lax.rsqrt(ms + EPS) * ng_ref[...]).astype(jnp.bfloat16)

    def row_chunks():
        for c in range(tm // IN_CHUNK_M):
            rows = slice(c * IN_CHUNK_M, (c + 1) * IN_CHUNK_M)
            yield c, rows, jnp.dot(h_ref[rows, :], w_ref[...],
                                   preferred_element_type=jnp.float32)

    @pl.when(j == W_Q)
    def _():
        scale = LOG2_E / (HEAD_DIM ** 0.5)
        for _, rows, acc in row_chunks():
            for hl in head_lanes:
                t = _head_norm_rope(acc[:, hl], qg_ref[...], cos_ref[rows, :], sin_ref[rows, :])
                p_ref[rows, hl] = (t * scale).astype(jnp.bfloat16)

    @pl.when(j == W_K)
    def _():
        for c, rows, acc in row_chunks():
            for hl in head_lanes:
                t = _head_norm_rope(acc[:, hl], kg_ref[...], cos_ref[rows, :], sin_ref[rows, :])
                p_ref[rows, hl] = t.astype(jnp.bfloat16)
                kmean_ref[0, c:c + 1, hl] = jnp.mean(t, axis=0, keepdims=True)

    @pl.when(j == W_V)
    def _():
        for c, _, acc in row_chunks():
            vt_ref[c] = acc.T.astype(jnp.bfloat16)

    @pl.when((j == W_ZA) | (j == W_ZC))
    def _():
        for _, rows, acc in row_chunks():
            p_ref[rows, :] = (acc * jax.nn.sigmoid(acc)).astype(jnp.bfloat16)

    @pl.when(j == W_UA)
    def _():
        for _, rows, acc in row_chunks():
            ua_ref[rows, :] = acc

    @pl.when(j == W_UB)
    def _():
        for _, rows, acc in row_chunks():
            p_ref[rows, :] = (ua_ref[rows, :] * jax.nn.sigmoid(acc)).astype(jnp.bfloat16)

    @pl.when(j >= W_GLA)
    def _():
        for _, rows, acc in row_chunks():
            p_ref[rows, :] = jax.nn.sigmoid(acc + bg_ref[...]).astype(jnp.bfloat16)


def _in_proj(x, norm_g, w_in, b_gate, q_g, k_g, cos, sin, seq):
    m = x.shape[0]
    tm = IN_TILE_M
    assert IN_CHUNK_M == MOBA_BLOCK
    n_seq_tiles = seq // tm
    blocks_per_tile = tm // MOBA_BLOCK

    def p_col(i, j):
        return (i, j - (j >= W_V).astype(jnp.int32) - (j >= W_UB).astype(jnp.int32))

    return pl.pallas_call(
        _in_proj_kernel,
        grid=(m // tm, N_W_GROUPS),
        in_specs=[
            pl.BlockSpec((tm, D_MODEL), lambda i, j: (i, 0)),
            pl.BlockSpec((1, D_MODEL), lambda i, j: (0, 0)),
            pl.BlockSpec((D_MODEL, D_MODEL), lambda i, j: (0, j)),
            pl.BlockSpec((1, D_MODEL), lambda i, j: (0, jnp.clip(j - W_GLA, 0, 1))),
            pl.BlockSpec((1, HEAD_DIM), lambda i, j: (0, 0)),
            pl.BlockSpec((1, HEAD_DIM), lambda i, j: (0, 0)),
            pl.BlockSpec((tm, HEAD_DIM), lambda i, j: (i % n_seq_tiles, 0)),
            pl.BlockSpec((tm, HEAD_DIM), lambda i, j: (i % n_seq_tiles, 0)),
        ],
        out_specs=[
            pl.BlockSpec((tm, D_MODEL), p_col),
            pl.BlockSpec((1, blocks_per_tile, D_MODEL), lambda i, j: (i, 0, 0)),
            pl.BlockSpec((blocks_per_tile, D_MODEL, MOBA_BLOCK), lambda i, j: (i, 0, 0)),
        ],
        out_shape=[
            jax.ShapeDtypeStruct((m, N_P_GROUPS * D_MODEL), jnp.bfloat16),
            jax.ShapeDtypeStruct((m // tm, blocks_per_tile, D_MODEL), jnp.float32),
            jax.ShapeDtypeStruct((m // MOBA_BLOCK, D_MODEL, MOBA_BLOCK), jnp.bfloat16),
        ],
        scratch_shapes=[
            pltpu.VMEM((tm, D_MODEL), jnp.bfloat16),
            pltpu.VMEM((tm, D_MODEL), jnp.float32),
        ],
        compiler_params=pltpu.CompilerParams(
            dimension_semantics=("arbitrary", "arbitrary"),
            vmem_limit_bytes=VMEM_LIMIT),
        name="in_proj",
    )(x, norm_g, w_in, b_gate, q_g, k_g, cos, sin)


def _attn_kernel(q_ref, k_ref, vt_ref, km_ref, o_ref, bias_ref, m_ref, acc_ref):
    qi = pl.program_id(1)
    n_blocks = km_ref.shape[1]
    tq = q_ref.shape[0]
    ones_rows = jnp.ones((ACC_ROWS - HEAD_DIM, MOBA_BLOCK), jnp.bfloat16)
    head_lanes = [slice(h * HEAD_DIM, (h + 1) * HEAD_DIM) for h in range(N_HEADS)]

    def weighted_values(h, j, p):
        v_aug = jnp.concatenate([vt_ref[j, head_lanes[h], :], ones_rows], axis=0)
        return jnp.dot(v_aug, p.astype(jnp.bfloat16), preferred_element_type=jnp.float32)

    blk_iota = lax.broadcasted_iota(jnp.int32, (n_blocks, tq), 0)
    past = blk_iota < qi
    key_i = lax.broadcasted_iota(jnp.int32, (MOBA_BLOCK, tq), 0)
    qry_i = lax.broadcasted_iota(jnp.int32, (MOBA_BLOCK, tq), 1)
    own = pl.multiple_of(qi * MOBA_BLOCK, MOBA_BLOCK)

    def scores(h, start):
        return lax.dot_general(k_ref[pl.ds(start, MOBA_BLOCK), head_lanes[h]],
                               q_ref[:, head_lanes[h]], _NT,
                               preferred_element_type=jnp.float32)

    gates = [lax.dot_general(km_ref[0, :, head_lanes[h]],
                             q_ref[:, head_lanes[h]].astype(jnp.float32), _NT,
                             preferred_element_type=jnp.float32) for h in range(N_HEADS)]
    for h in range(N_HEADS):
        g = jnp.where(past, gates[h], -jnp.inf)
        chosen = jnp.zeros((n_blocks, tq), jnp.float32)
        for _ in range(MOBA_TOPK):
            top = jnp.max(g, axis=0, keepdims=True)
            first = jnp.min(jnp.where(g == top, blk_iota, n_blocks), axis=0, keepdims=True)
            pick = blk_iota == first
            chosen = jnp.where(pick, 1.0, chosen)
            g = jnp.where(pick, -jnp.inf, g)
        bias_ref[h] = jnp.where(past, jnp.where(chosen > 0.0, 0.0, MASKED), MASKED)


    s_next = scores(0, own)
    for h in range(N_HEADS):
        s_t = s_next
        if h + 1 < N_HEADS:
            s_next = scores(h + 1, own)
        s_t = jnp.where(key_i <= qry_i, s_t, MASKED)
        m0 = jnp.max(s_t, axis=0, keepdims=True)
        m_ref[h] = m0
        acc_ref[h] = weighted_values(h, qi, jnp.exp2(s_t - m0))

    def body(j, carry):
        start = pl.multiple_of(j * MOBA_BLOCK, MOBA_BLOCK)
        s_next = scores(0, start)
        for h in range(N_HEADS):
            s_t = s_next
            if h + 1 < N_HEADS:
                s_next = scores(h + 1, start)
            bias = bias_ref[h, pl.ds(j, 1), :]
            m_i = m_ref[h]
            m_new = jnp.maximum(m_i, jnp.max(s_t, axis=0, keepdims=True) + bias)
            alpha = jnp.exp2(m_i - m_new)
            p = jnp.exp2(s_t - (m_new - bias))
            m_ref[h] = m_new
            acc_ref[h] = alpha * acc_ref[h] + weighted_values(h, j, p)
        return carry

    lax.fori_loop(0, qi, body, 0)

    for h in range(N_HEADS):
        acc = acc_ref[h]
        o_t = acc[:HEAD_DIM, :] / acc[HEAD_DIM:HEAD_DIM + 1, :]
        o_ref[:, head_lanes[h]] = o_t.T.astype(o_ref.dtype)


def _attention(p_arr, vt, kmean, batch, seq):
    m = batch * seq
    n_blocks = seq // MOBA_BLOCK
    tq = MOBA_BLOCK
    width = N_HEADS * HEAD_DIM
    resident = pl.Buffered(1)
    return pl.pallas_call(
        _attn_kernel,
        grid=(batch, n_blocks),
        in_specs=[
            pl.BlockSpec((tq, width), lambda b, i: (b * n_blocks + i, P_Q)),
            pl.BlockSpec((seq, width), lambda b, i: (b, P_K), pipeline_mode=resident),
            pl.BlockSpec((n_blocks, width, MOBA_BLOCK), lambda b, i: (b, 0, 0),
                         pipeline_mode=resident),
            pl.BlockSpec((1, n_blocks, width), lambda b, i: (b, 0, 0)),
        ],
        out_specs=pl.BlockSpec((tq, width), lambda b, i: (b * n_blocks + i, 0)),
        out_shape=jax.ShapeDtypeStruct((m, width), jnp.bfloat16),
        scratch_shapes=[
            pltpu.VMEM((N_HEADS, n_blocks, tq), jnp.float32),
            pltpu.VMEM((N_HEADS, 1, tq), jnp.float32),
            pltpu.VMEM((N_HEADS, ACC_ROWS, tq), jnp.float32),
        ],
        compiler_params=pltpu.CompilerParams(
            dimension_semantics=("arbitrary", "arbitrary"),
            vmem_limit_bytes=VMEM_LIMIT),
        name="moba_attn",
    )(p_arr, p_arr, vt, kmean)


def _tail_kernel(glu_ref, szc_ref, o_ref, sza_ref, ga_ref, gc_ref, x_ref,
                 wa_ref, wc_ref, wo_ref, cw_ref, cb_ref, cng_ref, cnb_ref,
                 out_ref, hist_ref, c_ref):
    s = pl.program_id(1)
    ts = x_ref.shape[0]
    n_lane_tiles = D_MODEL // LANES
    lane_tiles = [slice(c * LANES, (c + 1) * LANES) for c in range(n_lane_tiles)]

    @pl.when(s == 0)
    def _():
        for c in range(n_lane_tiles):
            hist_ref[c, 0:CONV_HALO, :] = jnp.zeros((CONV_HALO, LANES), jnp.float32)

    @pl.when(s > 0)
    def _():
        for c in range(n_lane_tiles):
            hist_ref[c, 0:CONV_HALO, :] = hist_ref[c, ts:ts + CONV_HALO, :]

    for c in range(n_lane_tiles):
        hist_ref[c, CONV_HALO:, :] = glu_ref[:, lane_tiles[c]].astype(jnp.float32)

    first_tap = CONV_HALO - (CONV_KERNEL - 1)

    def chunk(ci, carry):
        r0 = pl.multiple_of(ci * CONV_ROWS, CONV_ROWS)
        parts = []
        for c in range(n_lane_tiles):
            acc = hist_ref[c, pl.ds(r0 + first_tap, CONV_ROWS), :] * cw_ref[0:1, lane_tiles[c]]
            for t in range(1, CONV_KERNEL):
                acc = acc + (hist_ref[c, pl.ds(r0 + first_tap + t, CONV_ROWS), :]
                             * cw_ref[t:t + 1, lane_tiles[c]])
            parts.append(acc)
        conv = jnp.concatenate(parts, axis=1) + cb_ref[...]
        mu = jnp.mean(conv, axis=-1, keepdims=True)
        d = conv - mu
        var = jnp.mean(d * d, axis=-1, keepdims=True)
        y = d * lax.rsqrt(var + EPS) * cng_ref[...] + cnb_ref[...]
        act = y * jax.nn.sigmoid(y)
        gate = szc_ref[pl.ds(r0, CONV_ROWS), :].astype(jnp.float32)
        c_ref[pl.ds(r0, CONV_ROWS), :] = (act * gate).astype(jnp.bfloat16)
        return carry

    lax.fori_loop(0, ts // CONV_ROWS, chunk, 0, unroll=2)

    y_c = jnp.dot(c_ref[...], wc_ref[...], preferred_element_type=jnp.float32)
    a_in = (o_ref[...].astype(jnp.float32) * sza_ref[...].astype(jnp.float32)).astype(jnp.bfloat16)
    y_a = jnp.dot(a_in, wa_ref[...], preferred_element_type=jnp.float32)
    merged = (ga_ref[...].astype(jnp.float32) * y_a
              + gc_ref[...].astype(jnp.float32) * y_c).astype(jnp.bfloat16)
    out_ref[...] = x_ref[...] + jnp.dot(merged, wo_ref[...], preferred_element_type=jnp.float32)


def _tail(p_arr, o, x, wa, wc, wo, conv_w, conv_b, cn_g, cn_b, batch, seq):
    m = batch * seq
    ts = TAIL_TILE_M
    n_s = seq // ts

    def p_block(group):
        return pl.BlockSpec((ts, D_MODEL), lambda b, s: (b * n_s + s, group))

    row_block = pl.BlockSpec((ts, D_MODEL), lambda b, s: (b * n_s + s, 0))
    full = lambda shape: pl.BlockSpec(shape, lambda b, s: (0, 0))
    return pl.pallas_call(
        _tail_kernel,
        grid=(batch, n_s),
        in_specs=[
            p_block(P_GLU), p_block(P_SZC), row_block, p_block(P_SZA), p_block(P_GA),
            p_block(P_GC), row_block,
            full((D_MODEL, D_MODEL)), full((D_MODEL, D_MODEL)), full((D_MODEL, D_MODEL)),
            full((CONV_KERNEL, D_MODEL)), full((1, D_MODEL)), full((1, D_MODEL)),
            full((1, D_MODEL)),
        ],
        out_specs=row_block,
        out_shape=jax.ShapeDtypeStruct((m, D_MODEL), jnp.float32),
        scratch_shapes=[
            pltpu.VMEM((D_MODEL // LANES, CONV_HALO + ts, LANES), jnp.float32),
            pltpu.VMEM((ts, D_MODEL), jnp.bfloat16),
        ],
        compiler_params=pltpu.CompilerParams(
            dimension_semantics=("arbitrary", "arbitrary"),
            vmem_limit_bytes=VMEM_LIMIT),
        name="tail",
    )(p_arr, p_arr, o, p_arr, p_arr, p_arr, x, wa, wc, wo, conv_w, conv_b, cn_g, cn_b)


def _rope_tables(seq):
    inv_freq = ROPE_THETA ** (-jnp.arange(0, 2 * ROT_HALF, 2, dtype=jnp.float32) / (2 * ROT_HALF))
    ang = jnp.arange(seq, dtype=jnp.float32)[:, None] * inv_freq[None, :]
    cos, sin = jnp.cos(ang), jnp.sin(ang)
    plain = _HALF - ROT_HALF
    ones = jnp.ones((seq, plain), jnp.float32)
    zeros = jnp.zeros((seq, plain), jnp.float32)
    cos_t = jnp.concatenate([cos, ones, cos, ones], axis=1)
    sin_t = jnp.concatenate([-sin, zeros, sin, zeros], axis=1)
    return cos_t, sin_t


def _prepare_w_in(w):
    qk = w[:, :2 * D_MODEL].reshape(D_MODEL, 2 * N_HEADS, HEAD_DIM)
    qk = _permute_head_dims(qk).reshape(D_MODEL, 2 * D_MODEL)
    return jnp.concatenate([qk, w[:, 2 * D_MODEL:]], axis=1).astype(jnp.bfloat16)


def kernel(x, norm_g, w_in, b_gate, q_norm_g, k_norm_g, conv_w, conv_b, cn_g, cn_b,
           w_attn_proj, w_conv_proj, w_out):
    batch, seq, d = x.shape
    depth = w_in.shape[0]
    assert d == D_MODEL and seq % IN_TILE_M == 0 and seq % TAIL_TILE_M == 0
    cos_t, sin_t = _rope_tables(seq)
    bf = jnp.bfloat16
    xf = x.reshape(batch * seq, d)
    for l in range(depth):
        p_arr, kmean, vt = _in_proj(
            xf, norm_g[l][None], _prepare_w_in(w_in[l]), b_gate[l][None],
            _permute_head_dims(q_norm_g[l])[None], _permute_head_dims(k_norm_g[l])[None],
            cos_t, sin_t, seq)
        kmean = kmean.reshape(batch, seq // MOBA_BLOCK, D_MODEL)
        o = _attention(p_arr, vt, kmean, batch, seq)
        xf = _tail(p_arr, o, xf, w_attn_proj[l].astype(bf), w_conv_proj[l].astype(bf),
                   w_out[l].astype(bf), conv_w[l], conv_b[l][None], cn_g[l][None],
                   cn_b[l][None], batch, seq)
    return xf.reshape(batch, seq, d)
```

```python
import jax
import jax.numpy as jnp
from jax import lax
from jax.experimental import pallas as pl
from jax.experimental.pallas import tpu as pltpu

D_MODEL = 1024
N_HEADS = 8
HEAD_DIM = 128
ROT_HALF = 16
ROPE_THETA = 500000.0
MOBA_BLOCK = 256
MOBA_TOPK = 3
CONV_KERNEL = 31
CONV_HALO = 32
EPS = 1e-6
MASKED = -1e30
LOG2_E = 1.4426950408889634
ACC_ROWS = HEAD_DIM + 16
FIXED_SHIFT_MAX = 60.0

LANES = 128
IN_TILE_M = 1024
IN_CHUNK_M = MOBA_BLOCK
TAIL_TILE_M = 512
CONV_ROWS = 32
VMEM_LIMIT = 56 * 1024 * 1024

P_Q, P_K, P_SZA, P_GLU, P_SZC, P_GA, P_GC = range(7)
N_P_GROUPS = 7
W_Q, W_K, W_V, W_ZA, W_UA, W_UB, W_ZC, W_GLA, W_GLC = range(9)
N_W_GROUPS = 9

_NT = (((1,), (1,)), ((), ()))

_HALF = HEAD_DIM // 2
_HEAD_PERM_SEGMENTS = ((0, ROT_HALF), (2 * ROT_HALF, _HALF + ROT_HALF),
                       (ROT_HALF, 2 * ROT_HALF), (_HALF + ROT_HALF, HEAD_DIM))


def _permute_head_dims(t):
    return jnp.concatenate([t[..., a:b] for a, b in _HEAD_PERM_SEGMENTS], axis=-1)


def _head_norm_rope(t, g, cos, sin):
    ms = jnp.mean(t * t, axis=-1, keepdims=True)
    tn = t * lax.rsqrt(ms + EPS) * g
    return tn * cos + pltpu.roll(tn, _HALF, 1) * sin


def _in_proj_kernel(x_ref, ng_ref, w_ref, bg_ref, qg_ref, kg_ref, cos_ref, sin_ref,
                    p_ref, kmean_ref, vt_ref, h_ref, ua_ref):
    j = pl.program_id(1)
    tm = x_ref.shape[0]
    head_lanes = [slice(h * HEAD_DIM, (h + 1) * HEAD_DIM) for h in range(N_HEADS)]

    @pl.when(j == 0)
    def _():
        x = x_ref[...]
        ms = jnp.mean(x * x, axis=-1, keepdims=True)
        h_ref[...] = (x * lax.rsqrt(ms + EPS) * ng_ref[...]).astype(jnp.bfloat16)

    def row_chunks():
        for c in range(tm // IN_CHUNK_M):
            rows = slice(c * IN_CHUNK_M, (c + 1) * IN_CHUNK_M)
            yield c, rows, jnp.dot(h_ref[rows, :], w_ref[...],
                                   preferred_element_type=jnp.float32)

    @pl.when(j == W_Q)
    def _():
        scale = LOG2_E / (HEAD_DIM ** 0.5)
        for _, rows, acc in row_chunks():
            for hl in head_lanes:
                t = _head_norm_rope(acc[:, hl], qg_ref[...], cos_ref[rows, :], sin_ref[rows, :])
                p_ref[rows, hl] = (t * scale).astype(jnp.bfloat16)

    @pl.when(j == W_K)
    def _():
        for c, rows, acc in row_chunks():
            for hl in head_lanes:
                t = _head_norm_rope(acc[:, hl], kg_ref[...], cos_ref[rows, :], sin_ref[rows, :])
                p_ref[rows, hl] = t.astype(jnp.bfloat16)
                kmean_ref[0, c:c + 1, hl] = jnp.mean(t, axis=0, keepdims=True)

    @pl.when(j == W_V)
    def _():
        for c, _, acc in row_chunks():
            vt_ref[c] = acc.T.astype(jnp.bfloat16)

    @pl.when((j == W_ZA) | (j == W_ZC))
    def _():
        for _, rows, acc in row_chunks():
            p_ref[rows, :] = (acc * jax.nn.sigmoid(acc)).astype(jnp.bfloat16)

    @pl.when(j == W_UA)
    def _():
        for _, rows, acc in row_chunks():
            ua_ref[rows, :] = acc

    @pl.when(j == W_UB)
    def _():
        for _, rows, acc in row_chunks():
            p_ref[rows, :] = (ua_ref[rows, :] * jax.nn.sigmoid(acc)).astype(jnp.bfloat16)

    @pl.when(j >= W_GLA)
    def _():
        for _, rows, acc in row_chunks():
            p_ref[rows, :] = jax.nn.sigmoid(acc + bg_ref[...]).astype(jnp.bfloat16)


def _in_proj(x, norm_g, w_in, b_gate, q_g, k_g, cos, sin, seq):
    m = x.shape[0]
    tm = IN_TILE_M
    assert IN_CHUNK_M == MOBA_BLOCK
    n_seq_tiles = seq // tm
    blocks_per_tile = tm // MOBA_BLOCK

    def p_col(i, j):
        return (i, j - (j >= W_V).astype(jnp.int32) - (j >= W_UB).astype(jnp.int32))

    return pl.pallas_call(
        _in_proj_kernel,
        grid=(m // tm, N_W_GROUPS),
        in_specs=[
            pl.BlockSpec((tm, D_MODEL), lambda i, j: (i, 0)),
            pl.BlockSpec((1, D_MODEL), lambda i, j: (0, 0)),
            pl.BlockSpec((D_MODEL, D_MODEL), lambda i, j: (0, j)),
            pl.BlockSpec((1, D_MODEL), lambda i, j: (0, jnp.clip(j - W_GLA, 0, 1))),
            pl.BlockSpec((1, HEAD_DIM), lambda i, j: (0, 0)),
            pl.BlockSpec((1, HEAD_DIM), lambda i, j: (0, 0)),
            pl.BlockSpec((tm, HEAD_DIM), lambda i, j: (i % n_seq_tiles, 0)),
            pl.BlockSpec((tm, HEAD_DIM), lambda i, j: (i % n_seq_tiles, 0)),
        ],
        out_specs=[
            pl.BlockSpec((tm, D_MODEL), p_col),
            pl.BlockSpec((1, blocks_per_tile, D_MODEL), lambda i, j: (i, 0, 0)),
            pl.BlockSpec((blocks_per_tile, D_MODEL, MOBA_BLOCK), lambda i, j: (i, 0, 0)),
        ],
        out_shape=[
            jax.ShapeDtypeStruct((m, N_P_GROUPS * D_MODEL), jnp.bfloat16),
            jax.ShapeDtypeStruct((m // tm, blocks_per_tile, D_MODEL), jnp.float32),
            jax.ShapeDtypeStruct((m // MOBA_BLOCK, D_MODEL, MOBA_BLOCK), jnp.bfloat16),
        ],
        scratch_shapes=[
            pltpu.VMEM((tm, D_MODEL), jnp.bfloat16),
            pltpu.VMEM((tm, D_MODEL), jnp.float32),
        ],
        compiler_params=pltpu.CompilerParams(
            dimension_semantics=("arbitrary", "arbitrary"),
            vmem_limit_bytes=VMEM_LIMIT),
        name="in_proj",
    )(x, norm_g, w_in, b_gate, q_g, k_g, cos, sin)


def _attn_kernel(bound_ref, q_ref, k_ref, vt_ref, km_ref, o_ref, bias_ref, m_ref, acc_ref):
    qi = pl.program_id(1)
    n_blocks = km_ref.shape[1]
    tq = q_ref.shape[0]
    ones_rows = jnp.ones((ACC_ROWS - HEAD_DIM, MOBA_BLOCK), jnp.bfloat16)
    head_lanes = [slice(h * HEAD_DIM, (h + 1) * HEAD_DIM) for h in range(N_HEADS)]
    bound = bound_ref[0]
    fixed_shift = bound <= FIXED_SHIFT_MAX

    def weighted_values(h, j, p):
        v_aug = jnp.concatenate([vt_ref[j, head_lanes[h], :], ones_rows], axis=0)
        return jnp.dot(v_aug, p.astype(jnp.bfloat16), preferred_element_type=jnp.float32)

    def scores(h, start):
        return lax.dot_general(k_ref[pl.ds(start, MOBA_BLOCK), head_lanes[h]],
                               q_ref[:, head_lanes[h]], _NT,
                               preferred_element_type=jnp.float32)

    def for_each_head(start, consume):
        ahead = [scores(0, start), scores(1, start)]
        for h in range(N_HEADS):
            s_t = ahead.pop(0)
            if h + 2 < N_HEADS:
                ahead.append(scores(h + 2, start))
            consume(h, s_t)

    blk_iota = lax.broadcasted_iota(jnp.int32, (n_blocks, tq), 0)
    past = blk_iota < qi
    gates = [lax.dot_general(km_ref[0, :, head_lanes[h]],
                             q_ref[:, head_lanes[h]].astype(jnp.float32), _NT,
                             preferred_element_type=jnp.float32) for h in range(N_HEADS)]
    for h in range(N_HEADS):
        g = jnp.where(past, gates[h], -jnp.inf)
        chosen = jnp.zeros((n_blocks, tq), jnp.float32)
        for _ in range(MOBA_TOPK):
            top = jnp.max(g, axis=0, keepdims=True)
            first = jnp.min(jnp.where(g == top, blk_iota, n_blocks), axis=0, keepdims=True)
            pick = blk_iota == first
            chosen = jnp.where(pick, 1.0, chosen)
            g = jnp.where(pick, -jnp.inf, g)
        bias_ref[h] = jnp.where(past, jnp.where(chosen > 0.0, 0.0, MASKED), MASKED)

    key_i = lax.broadcasted_iota(jnp.int32, (MOBA_BLOCK, tq), 0)
    qry_i = lax.broadcasted_iota(jnp.int32, (MOBA_BLOCK, tq), 1)
    own = pl.multiple_of(qi * MOBA_BLOCK, MOBA_BLOCK)

    @pl.when(fixed_shift)
    def _():
        def own_block(h, s_t):
            p = jnp.exp2(jnp.where(key_i <= qry_i, s_t, MASKED) - bound)
            acc_ref[h] = weighted_values(h, qi, p)

        for_each_head(own, own_block)

        def probabilities(h, j, s_t):
            shift = bound - bias_ref[h, pl.ds(j, 1), :]
            return jnp.exp2(s_t - shift).astype(jnp.bfloat16)

        def group_step(j0, n):
            start = pl.multiple_of(j0 * MOBA_BLOCK, MOBA_BLOCK)
            ones = jnp.ones((ACC_ROWS - HEAD_DIM, n * MOBA_BLOCK), jnp.bfloat16)

            def group_scores(h):
                return [scores(h, start + b * MOBA_BLOCK) for b in range(n)]

            ahead = [group_scores(0)]
            for h in range(N_HEADS):
                s_blocks = ahead.pop(0)
                if h + 1 < N_HEADS:
                    ahead.append(group_scores(h + 1))
                p = jnp.concatenate([probabilities(h, j0 + b, s_blocks[b]) for b in range(n)],
                                    axis=0)
                v_blocks = jnp.concatenate([vt_ref[j0 + b, head_lanes[h], :] for b in range(n)],
                                           axis=1)
                v_aug = jnp.concatenate([v_blocks, ones], axis=0)
                acc_ref[h] = acc_ref[h] + jnp.dot(v_aug, p, preferred_element_type=jnp.float32)

        def quad_body(i, carry):
            group_step(4 * i, 4)
            return carry

        lax.fori_loop(0, qi // 4, quad_body, 0)
        rest = (qi // 4) * 4

        @pl.when(qi % 4 >= 2)
        def _():
            group_step(rest, 2)

        @pl.when(qi % 2 == 1)
        def _():
            group_step(qi - 1, 1)

    @pl.when(jnp.logical_not(fixed_shift))
    def _():
        def own_block(h, s_t):
            s_t = jnp.where(key_i <= qry_i, s_t, MASKED)
            m0 = jnp.max(s_t, axis=0, keepdims=True)
            m_ref[h] = m0
            acc_ref[h] = weighted_values(h, qi, jnp.exp2(s_t - m0))

        for_each_head(own, own_block)

        def body(j, carry):
            def past_block(h, s_t):
                bias = bias_ref[h, pl.ds(j, 1), :]
                m_i = m_ref[h]
                m_new = jnp.maximum(m_i, jnp.max(s_t, axis=0, keepdims=True) + bias)
                alpha = jnp.exp2(m_i - m_new)
                p = jnp.exp2(s_t - (m_new - bias))
                m_ref[h] = m_new
                acc_ref[h] = alpha * acc_ref[h] + weighted_values(h, j, p)

            for_each_head(pl.multiple_of(j * MOBA_BLOCK, MOBA_BLOCK), past_block)
            return carry

        lax.fori_loop(0, qi, body, 0)

    for h in range(N_HEADS):
        acc = acc_ref[h]
        o_t = acc[:HEAD_DIM, :] / acc[HEAD_DIM:HEAD_DIM + 1, :]
        o_ref[:, head_lanes[h]] = o_t.T.astype(o_ref.dtype)


def _attention(score_bound, p_arr, vt, kmean, batch, seq):
    m = batch * seq
    n_blocks = seq // MOBA_BLOCK
    tq = MOBA_BLOCK
    width = N_HEADS * HEAD_DIM
    resident = pl.Buffered(1)
    return pl.pallas_call(
        _attn_kernel,
        grid=(batch, n_blocks),
        in_specs=[
            pl.BlockSpec(memory_space=pltpu.SMEM),
            pl.BlockSpec((tq, width), lambda b, i: (b * n_blocks + i, P_Q)),
            pl.BlockSpec((seq, width), lambda b, i: (b, P_K), pipeline_mode=resident),
            pl.BlockSpec((n_blocks, width, MOBA_BLOCK), lambda b, i: (b, 0, 0),
                         pipeline_mode=resident),
            pl.BlockSpec((1, n_blocks, width), lambda b, i: (b, 0, 0)),
        ],
        out_specs=pl.BlockSpec((tq, width), lambda b, i: (b * n_blocks + i, 0)),
        out_shape=jax.ShapeDtypeStruct((m, width), jnp.bfloat16),
        scratch_shapes=[
            pltpu.VMEM((N_HEADS, n_blocks, tq), jnp.float32),
            pltpu.VMEM((N_HEADS, 1, tq), jnp.float32),
            pltpu.VMEM((N_HEADS, ACC_ROWS, tq), jnp.float32),
        ],
        compiler_params=pltpu.CompilerParams(
            dimension_semantics=("arbitrary", "arbitrary"),
            vmem_limit_bytes=VMEM_LIMIT),
        name="moba_attn",
    )(score_bound, p_arr, p_arr, vt, kmean)


def _tail_kernel(glu_ref, szc_ref, o_ref, sza_ref, ga_ref, gc_ref, x_ref,
                 wa_ref, wc_ref, wo_ref, cw_ref, cb_ref, cng_ref, cnb_ref,
                 out_ref, hist_ref, c_ref):
    s = pl.program_id(1)
    ts = x_ref.shape[0]
    n_lane_tiles = D_MODEL // LANES
    lane_tiles = [slice(c * LANES, (c + 1) * LANES) for c in range(n_lane_tiles)]

    @pl.when(s == 0)
    def _():
        for c in range(n_lane_tiles):
            hist_ref[c, 0:CONV_HALO, :] = jnp.zeros((CONV_HALO, LANES), jnp.float32)

    @pl.when(s > 0)
    def _():
        for c in range(n_lane_tiles):
            hist_ref[c, 0:CONV_HALO, :] = hist_ref[c, ts:ts + CONV_HALO, :]

    for c in range(n_lane_tiles):
        hist_ref[c, CONV_HALO:, :] = glu_ref[:, lane_tiles[c]].astype(jnp.float32)

    first_tap = CONV_HALO - (CONV_KERNEL - 1)

    def chunk(ci, carry):
        r0 = pl.multiple_of(ci * CONV_ROWS, CONV_ROWS)
        parts = []
        for c in range(n_lane_tiles):
            acc = hist_ref[c, pl.ds(r0 + first_tap, CONV_ROWS), :] * cw_ref[0:1, lane_tiles[c]]
            for t in range(1, CONV_KERNEL):
                acc = acc + (hist_ref[c, pl.ds(r0 + first_tap + t, CONV_ROWS), :]
                             * cw_ref[t:t + 1, lane_tiles[c]])
            parts.append(acc)
        conv = jnp.concatenate(parts, axis=1) + cb_ref[...]
        mu = jnp.mean(conv, axis=-1, keepdims=True)
        d = conv - mu
        var = jnp.mean(d * d, axis=-1, keepdims=True)
        y = d * lax.rsqrt(var + EPS) * cng_ref[...] + cnb_ref[...]
        act = y * jax.nn.sigmoid(y)
        gate = szc_ref[pl.ds(r0, CONV_ROWS), :].astype(jnp.float32)
        c_ref[pl.ds(r0, CONV_ROWS), :] = (act * gate).astype(jnp.bfloat16)
        return carry

    lax.fori_loop(0, ts // CONV_ROWS, chunk, 0, unroll=2)

    y_c = jnp.dot(c_ref[...], wc_ref[...], preferred_element_type=jnp.float32)
    a_in = (o_ref[...].astype(jnp.float32) * sza_ref[...].astype(jnp.float32)).astype(jnp.bfloat16)
    y_a = jnp.dot(a_in, wa_ref[...], preferred_element_type=jnp.float32)
    merged = (ga_ref[...].astype(jnp.float32) * y_a
              + gc_ref[...].astype(jnp.float32) * y_c).astype(jnp.bfloat16)
    out_ref[...] = x_ref[...] + jnp.dot(merged, wo_ref[...], preferred_element_type=jnp.float32)


def _tail(p_arr, o, x, wa, wc, wo, conv_w, conv_b, cn_g, cn_b, batch, seq):
    m = batch * seq
    ts = TAIL_TILE_M
    n_s = seq // ts

    def p_block(group):
        return pl.BlockSpec((ts, D_MODEL), lambda b, s: (b * n_s + s, group))

    row_block = pl.BlockSpec((ts, D_MODEL), lambda b, s: (b * n_s + s, 0))
    full = lambda shape: pl.BlockSpec(shape, lambda b, s: (0, 0))
    return pl.pallas_call(
        _tail_kernel,
        grid=(batch, n_s),
        in_specs=[
            p_block(P_GLU), p_block(P_SZC), row_block, p_block(P_SZA), p_block(P_GA),
            p_block(P_GC), row_block,
            full((D_MODEL, D_MODEL)), full((D_MODEL, D_MODEL)), full((D_MODEL, D_MODEL)),
            full((CONV_KERNEL, D_MODEL)), full((1, D_MODEL)), full((1, D_MODEL)),
            full((1, D_MODEL)),
        ],
        out_specs=row_block,
        out_shape=jax.ShapeDtypeStruct((m, D_MODEL), jnp.float32),
        scratch_shapes=[
            pltpu.VMEM((D_MODEL // LANES, CONV_HALO + ts, LANES), jnp.float32),
            pltpu.VMEM((ts, D_MODEL), jnp.bfloat16),
        ],
        compiler_params=pltpu.CompilerParams(
            dimension_semantics=("arbitrary", "arbitrary"),
            vmem_limit_bytes=VMEM_LIMIT),
        name="tail",
    )(p_arr, p_arr, o, p_arr, p_arr, p_arr, x, wa, wc, wo, conv_w, conv_b, cn_g, cn_b)


def _rope_tables(seq):
    inv_freq = ROPE_THETA ** (-jnp.arange(0, 2 * ROT_HALF, 2, dtype=jnp.float32) / (2 * ROT_HALF))
    plain = jnp.zeros((_HALF - ROT_HALF,), jnp.float32)
    lane_freq = jnp.concatenate([inv_freq, plain, inv_freq, plain])
    lane_sign = jnp.concatenate([-jnp.ones((_HALF,), jnp.float32), jnp.ones((_HALF,), jnp.float32)])
    ang = jnp.arange(seq, dtype=jnp.float32)[:, None] * lane_freq[None, :]
    return jnp.cos(ang), jnp.sin(ang) * lane_sign[None, :]


def _prepare_w_in(w):
    qk = w[:, :2 * D_MODEL].reshape(D_MODEL, 2 * N_HEADS, HEAD_DIM)
    qk = _permute_head_dims(qk).reshape(D_MODEL, 2 * D_MODEL)
    return jnp.concatenate([qk, w[:, 2 * D_MODEL:]], axis=1).astype(jnp.bfloat16)


def kernel(x, norm_g, w_in, b_gate, q_norm_g, k_norm_g, conv_w, conv_b, cn_g, cn_b,
           w_attn_proj, w_conv_proj, w_out):
    batch, seq, d = x.shape
    depth = w_in.shape[0]
    assert d == D_MODEL and seq % IN_TILE_M == 0 and seq % TAIL_TILE_M == 0
    cos_t, sin_t = _rope_tables(seq)
    bf = jnp.bfloat16
    xf = x.reshape(batch * seq, d)
    for l in range(depth):
        p_arr, kmean, vt = _in_proj(
            xf, norm_g[l][None], _prepare_w_in(w_in[l]), b_gate[l][None],
            _permute_head_dims(q_norm_g[l])[None], _permute_head_dims(k_norm_g[l])[None],
            cos_t, sin_t, seq)
        kmean = kmean.reshape(batch, seq // MOBA_BLOCK, D_MODEL)
        score_bound = (LOG2_E * HEAD_DIM ** 0.5 * jnp.max(jnp.abs(q_norm_g[l]))
                       * jnp.max(jnp.abs(k_norm_g[l]))).reshape(1)
        o = _attention(score_bound, p_arr, vt, kmean, batch, seq)
        xf = _tail(p_arr, o, xf, w_attn_proj[l].astype(bf), w_conv_proj[l].astype(bf),
                   w_out[l].astype(bf), conv_w[l], conv_b[l][None], cn_g[l][None],
                   cn_b[l][None], batch, seq)
    return xf.reshape(batch, seq, d)
```

```python
import jax
import jax.numpy as jnp
from jax import lax
from jax.experimental import pallas as pl
from jax.experimental.pallas import tpu as pltpu

D_MODEL = 1024
N_HEADS = 8
HEAD_DIM = 128
ROT_HALF = 16
ROPE_THETA = 500000.0
MOBA_BLOCK = 256
MOBA_TOPK = 3
CONV_KERNEL = 31
CONV_HALO = 32
EPS = 1e-6
MASKED = -1e30
LOG2_E = 1.4426950408889634
ACC_ROWS = HEAD_DIM + 16
FIXED_SHIFT_MAX = 60.0

LANES = 128
IN_TILE_M = 1024
IN_CHUNK_M = MOBA_BLOCK
TAIL_TILE_M = 512
CONV_ROWS = 32
TAIL_MM_ROWS = 256
VMEM_LIMIT = 56 * 1024 * 1024

P_Q, P_K, P_SZA, P_GLU, P_SZC, P_GA, P_GC = range(7)
N_P_GROUPS = 7
W_Q, W_K, W_V, W_ZA, W_UA, W_UB, W_ZC, W_GLA, W_GLC = range(9)
N_W_GROUPS = 9

_NT = (((1,), (1,)), ((), ()))

_HALF = HEAD_DIM // 2
_HEAD_PERM_SEGMENTS = ((0, ROT_HALF), (2 * ROT_HALF, _HALF + ROT_HALF),
                       (ROT_HALF, 2 * ROT_HALF), (_HALF + ROT_HALF, HEAD_DIM))


def _permute_head_dims(t):
    return jnp.concatenate([t[..., a:b] for a, b in _HEAD_PERM_SEGMENTS], axis=-1)


def _head_norm_rope(t, g, cos, sin):
    u = t * g
    rotated = u * cos + pltpu.roll(u, _HALF, 1) * sin
    ms = jnp.mean(t * t, axis=-1, keepdims=True)
    return rotated * lax.rsqrt(ms + EPS)


def _in_proj_kernel(x_ref, ng_ref, wqk_ref, w_ref, bg_ref, qg_ref, kg_ref, cos_ref, sin_ref,
                    p_ref, kmean_ref, vt_ref, h_ref, ua_ref, qacc_ref, kacc_ref):
    j = pl.program_id(1)
    tm = x_ref.shape[0]
    head_lanes = [slice(h * HEAD_DIM, (h + 1) * HEAD_DIM) for h in range(N_HEADS)]
    chunk_rows = [slice(c * IN_CHUNK_M, (c + 1) * IN_CHUNK_M) for c in range(tm // IN_CHUNK_M)]

    def row_chunks():
        for c, rows in enumerate(chunk_rows):
            yield c, rows, jnp.dot(h_ref[rows, :], w_ref[...], preferred_element_type=jnp.float32)

    def finish_q():
        scale = LOG2_E / (HEAD_DIM ** 0.5)
        for rows in chunk_rows:
            for hl in head_lanes:
                t = _head_norm_rope(qacc_ref[rows, hl], qg_ref[...], cos_ref[rows, :], sin_ref[rows, :])
                p_ref[rows, hl] = (t * scale).astype(jnp.bfloat16)

    def finish_k():
        for c, rows in enumerate(chunk_rows):
            for hl in head_lanes:
                t = _head_norm_rope(kacc_ref[rows, hl], kg_ref[...], cos_ref[rows, :], sin_ref[rows, :])
                p_ref[rows, hl] = t.astype(jnp.bfloat16)
                kmean_ref[0, c:c + 1, hl] = jnp.mean(t, axis=0, keepdims=True)

    @pl.when(j == W_Q)
    def _():
        x = x_ref[...]
        ms = jnp.mean(x * x, axis=-1, keepdims=True)
        h_ref[...] = (x * lax.rsqrt(ms + EPS) * ng_ref[...]).astype(jnp.bfloat16)
        qacc_ref[...] = jnp.dot(h_ref[...], wqk_ref[...], preferred_element_type=jnp.float32)

    @pl.when(j == W_K)
    def _():
        finish_q()
        kacc_ref[...] = jnp.dot(h_ref[...], wqk_ref[...], preferred_element_type=jnp.float32)

    @pl.when(j == W_V)
    def _():
        finish_k()
        for c, _, acc in row_chunks():
            vt_ref[c] = acc.T.astype(jnp.bfloat16)

    @pl.when((j == W_ZA) | (j == W_ZC))
    def _():
        for _, rows, acc in row_chunks():
            p_ref[rows, :] = (acc * jax.nn.sigmoid(acc)).astype(jnp.bfloat16)

    @pl.when(j == W_UA)
    def _():
        for _, rows, acc in row_chunks():
            ua_ref[rows, :] = acc

    @pl.when(j == W_UB)
    def _():
        for _, rows, acc in row_chunks():
            p_ref[rows, :] = (ua_ref[rows, :] * jax.nn.sigmoid(acc)).astype(jnp.bfloat16)

    @pl.when(j >= W_GLA)
    def _():
        for _, rows, acc in row_chunks():
            p_ref[rows, :] = jax.nn.sigmoid(acc + bg_ref[...]).astype(jnp.bfloat16)


def _in_proj(x, norm_g, w_qk, w_rest, b_gate, q_g, k_g, cos, sin, seq):
    m = x.shape[0]
    tm = IN_TILE_M
    assert IN_CHUNK_M == MOBA_BLOCK
    n_seq_tiles = seq // tm
    blocks_per_tile = tm // MOBA_BLOCK

    def p_col(i, j):
        return (i, j - (j >= W_K).astype(jnp.int32) - (j >= W_UB).astype(jnp.int32))

    return pl.pallas_call(
        _in_proj_kernel,
        grid=(m // tm, N_W_GROUPS),
        in_specs=[
            pl.BlockSpec((tm, D_MODEL), lambda i, j: (i, 0)),
            pl.BlockSpec((1, D_MODEL), lambda i, j: (0, 0)),
            pl.BlockSpec((D_MODEL, D_MODEL), lambda i, j: (0, jnp.minimum(j, W_K))),
            pl.BlockSpec((D_MODEL, D_MODEL), lambda i, j: (0, jnp.maximum(j - W_V, 0))),
            pl.BlockSpec((1, D_MODEL), lambda i, j: (0, jnp.clip(j - W_GLA, 0, 1))),
            pl.BlockSpec((1, HEAD_DIM), lambda i, j: (0, 0)),
            pl.BlockSpec((1, HEAD_DIM), lambda i, j: (0, 0)),
            pl.BlockSpec((tm, HEAD_DIM), lambda i, j: (i % n_seq_tiles, 0)),
            pl.BlockSpec((tm, HEAD_DIM), lambda i, j: (i % n_seq_tiles, 0)),
        ],
        out_specs=[
            pl.BlockSpec((tm, D_MODEL), p_col),
            pl.BlockSpec((1, blocks_per_tile, D_MODEL), lambda i, j: (i, 0, 0)),
            pl.BlockSpec((blocks_per_tile, D_MODEL, MOBA_BLOCK), lambda i, j: (i, 0, 0)),
        ],
        out_shape=[
            jax.ShapeDtypeStruct((m, N_P_GROUPS * D_MODEL), jnp.bfloat16),
            jax.ShapeDtypeStruct((m // tm, blocks_per_tile, D_MODEL), jnp.float32),
            jax.ShapeDtypeStruct((m // MOBA_BLOCK, D_MODEL, MOBA_BLOCK), jnp.bfloat16),
        ],
        scratch_shapes=[
            pltpu.VMEM((tm, D_MODEL), jnp.bfloat16),
            pltpu.VMEM((tm, D_MODEL), jnp.float32),
            pltpu.VMEM((tm, D_MODEL), jnp.float32),
            pltpu.VMEM((tm, D_MODEL), jnp.float32),
        ],
        compiler_params=pltpu.CompilerParams(
            dimension_semantics=("arbitrary", "arbitrary"),
            vmem_limit_bytes=VMEM_LIMIT),
        name="in_proj",
    )(x, norm_g, w_qk, w_rest, b_gate, q_g, k_g, cos, sin)


def _attn_kernel(bound_ref, q_ref, k_ref, vt_ref, km_ref, o_ref, bias_ref, m_ref, acc_ref):
    qi = pl.program_id(1)
    n_blocks = km_ref.shape[1]
    tq = q_ref.shape[0]
    ones_rows = jnp.ones((ACC_ROWS - HEAD_DIM, MOBA_BLOCK), jnp.bfloat16)
    head_lanes = [slice(h * HEAD_DIM, (h + 1) * HEAD_DIM) for h in range(N_HEADS)]
    bound = bound_ref[0]
    fixed_shift = bound <= FIXED_SHIFT_MAX

    def weighted_values(h, j, p):
        v_aug = jnp.concatenate([vt_ref[j, head_lanes[h], :], ones_rows], axis=0)
        return jnp.dot(v_aug, p.astype(jnp.bfloat16), preferred_element_type=jnp.float32)

    def scores(h, start):
        return lax.dot_general(k_ref[pl.ds(start, MOBA_BLOCK), head_lanes[h]],
                               q_ref[:, head_lanes[h]], _NT,
                               preferred_element_type=jnp.float32)

    def for_each_head(start, consume):
        ahead = [scores(0, start), scores(1, start)]
        for h in range(N_HEADS):
            s_t = ahead.pop(0)
            if h + 2 < N_HEADS:
                ahead.append(scores(h + 2, start))
            consume(h, s_t)

    blk_iota = lax.broadcasted_iota(jnp.int32, (n_blocks, tq), 0)
    past = blk_iota < qi
    gates = [lax.dot_general(km_ref[0, :, head_lanes[h]],
                             q_ref[:, head_lanes[h]].astype(jnp.float32), _NT,
                             preferred_element_type=jnp.float32) for h in range(N_HEADS)]
    for h in range(N_HEADS):
        g = jnp.where(past, gates[h], -jnp.inf)
        chosen = jnp.zeros((n_blocks, tq), jnp.float32)
        for _ in range(MOBA_TOPK):
            top = jnp.max(g, axis=0, keepdims=True)
            first = jnp.min(jnp.where(g == top, blk_iota, n_blocks), axis=0, keepdims=True)
            pick = blk_iota == first
            chosen = jnp.where(pick, 1.0, chosen)
            g = jnp.where(pick, -jnp.inf, g)
        bias_ref[h] = jnp.where(past, jnp.where(chosen > 0.0, 0.0, MASKED), MASKED)

    key_i = lax.broadcasted_iota(jnp.int32, (MOBA_BLOCK, tq), 0)
    qry_i = lax.broadcasted_iota(jnp.int32, (MOBA_BLOCK, tq), 1)
    own = pl.multiple_of(qi * MOBA_BLOCK, MOBA_BLOCK)

    @pl.when(fixed_shift)
    def _():
        def own_block(h, s_t):
            p = jnp.exp2(jnp.where(key_i <= qry_i, s_t, MASKED) - bound)
            acc_ref[h] = weighted_values(h, qi, p)

        for_each_head(own, own_block)

        def probabilities(h, j, s_t):
            shift = bound - bias_ref[h, pl.ds(j, 1), :]
            return jnp.exp2(s_t - shift).astype(jnp.bfloat16)

        def group_step(j0, n):
            start = pl.multiple_of(j0 * MOBA_BLOCK, MOBA_BLOCK)
            ones = jnp.ones((ACC_ROWS - HEAD_DIM, n * MOBA_BLOCK), jnp.bfloat16)

            def group_scores(h):
                return [scores(h, start + b * MOBA_BLOCK) for b in range(n)]

            ahead = [group_scores(0)]
            for h in range(N_HEADS):
                s_blocks = ahead.pop(0)
                if h + 1 < N_HEADS:
                    ahead.append(group_scores(h + 1))
                p = jnp.concatenate([probabilities(h, j0 + b, s_blocks[b]) for b in range(n)],
                                    axis=0)
                v_blocks = jnp.concatenate([vt_ref[j0 + b, head_lanes[h], :] for b in range(n)],
                                           axis=1)
                v_aug = jnp.concatenate([v_blocks, ones], axis=0)
                acc_ref[h] = acc_ref[h] + jnp.dot(v_aug, p, preferred_element_type=jnp.float32)

        def quad_body(i, carry):
            group_step(4 * i, 4)
            return carry

        lax.fori_loop(0, qi // 4, quad_body, 0)
        rest = (qi // 4) * 4

        @pl.when(qi % 4 >= 2)
        def _():
            group_step(rest, 2)

        @pl.when(qi % 2 == 1)
        def _():
            group_step(qi - 1, 1)

    @pl.when(jnp.logical_not(fixed_shift))
    def _():
        def own_block(h, s_t):
            s_t = jnp.where(key_i <= qry_i, s_t, MASKED)
            m0 = jnp.max(s_t, axis=0, keepdims=True)
            m_ref[h] = m0
            acc_ref[h] = weighted_values(h, qi, jnp.exp2(s_t - m0))

        for_each_head(own, own_block)

        def body(j, carry):
            def past_block(h, s_t):
                bias = bias_ref[h, pl.ds(j, 1), :]
                m_i = m_ref[h]
                m_new = jnp.maximum(m_i, jnp.max(s_t, axis=0, keepdims=True) + bias)
                alpha = jnp.exp2(m_i - m_new)
                p = jnp.exp2(s_t - (m_new - bias))
                m_ref[h] = m_new
                acc_ref[h] = alpha * acc_ref[h] + weighted_values(h, j, p)

            for_each_head(pl.multiple_of(j * MOBA_BLOCK, MOBA_BLOCK), past_block)
            return carry

        lax.fori_loop(0, qi, body, 0)

    for h in range(N_HEADS):
        acc = acc_ref[h]
        o_t = acc[:HEAD_DIM, :] / acc[HEAD_DIM:HEAD_DIM + 1, :]
        o_ref[:, head_lanes[h]] = o_t.T.astype(o_ref.dtype)


def _attention(score_bound, p_arr, vt, kmean, batch, seq):
    m = batch * seq
    n_blocks = seq // MOBA_BLOCK
    tq = MOBA_BLOCK
    width = N_HEADS * HEAD_DIM
    resident = pl.Buffered(1)
    return pl.pallas_call(
        _attn_kernel,
        grid=(batch, n_blocks),
        in_specs=[
            pl.BlockSpec(memory_space=pltpu.SMEM),
            pl.BlockSpec((tq, width), lambda b, i: (b * n_blocks + i, P_Q)),
            pl.BlockSpec((seq, width), lambda b, i: (b, P_K), pipeline_mode=resident),
            pl.BlockSpec((n_blocks, width, MOBA_BLOCK), lambda b, i: (b, 0, 0),
                         pipeline_mode=resident),
            pl.BlockSpec((1, n_blocks, width), lambda b, i: (b, 0, 0)),
        ],
        out_specs=pl.BlockSpec((tq, width), lambda b, i: (b * n_blocks + i, 0)),
        out_shape=jax.ShapeDtypeStruct((m, width), jnp.bfloat16),
        scratch_shapes=[
            pltpu.VMEM((N_HEADS, n_blocks, tq), jnp.float32),
            pltpu.VMEM((N_HEADS, 1, tq), jnp.float32),
            pltpu.VMEM((N_HEADS, ACC_ROWS, tq), jnp.float32),
        ],
        compiler_params=pltpu.CompilerParams(
            dimension_semantics=("arbitrary", "arbitrary"),
            vmem_limit_bytes=VMEM_LIMIT),
        name="moba_attn",
    )(score_bound, p_arr, p_arr, vt, kmean)


def _tail_kernel(glu_ref, szc_ref, o_ref, sza_ref, ga_ref, gc_ref, x_ref,
                 wa_ref, wc_ref, wo_ref, cw_ref, cb_ref, cng_ref, cnb_ref,
                 out_ref, hist_ref, c_ref):
    s = pl.program_id(1)
    ts = x_ref.shape[0]
    n_lane_tiles = D_MODEL // LANES
    lane_tiles = [slice(c * LANES, (c + 1) * LANES) for c in range(n_lane_tiles)]

    @pl.when(s == 0)
    def _():
        for c in range(n_lane_tiles):
            hist_ref[c, 0:CONV_HALO, :] = jnp.zeros((CONV_HALO, LANES), jnp.float32)

    @pl.when(s > 0)
    def _():
        for c in range(n_lane_tiles):
            hist_ref[c, 0:CONV_HALO, :] = hist_ref[c, ts:ts + CONV_HALO, :]

    for c in range(n_lane_tiles):
        hist_ref[c, CONV_HALO:, :] = glu_ref[:, lane_tiles[c]].astype(jnp.float32)

    first_tap = CONV_HALO - (CONV_KERNEL - 1)

    def conv_norm_chunk(r0):
        parts = []
        for c in range(n_lane_tiles):
            acc = hist_ref[c, r0 + first_tap:r0 + first_tap + CONV_ROWS, :] * cw_ref[0:1, lane_tiles[c]]
            for t in range(1, CONV_KERNEL):
                acc = acc + (hist_ref[c, r0 + first_tap + t:r0 + first_tap + t + CONV_ROWS, :]
                             * cw_ref[t:t + 1, lane_tiles[c]])
            parts.append(acc)
        conv = jnp.concatenate(parts, axis=1) + cb_ref[...]
        mu = jnp.mean(conv, axis=-1, keepdims=True)
        d = conv - mu
        var = jnp.mean(d * d, axis=-1, keepdims=True)
        y = d * lax.rsqrt(var + EPS) * cng_ref[...] + cnb_ref[...]
        act = y * jax.nn.sigmoid(y)
        gate = szc_ref[r0:r0 + CONV_ROWS, :].astype(jnp.float32)
        c_ref[r0:r0 + CONV_ROWS, :] = (act * gate).astype(jnp.bfloat16)

    for m0 in range(0, ts, TAIL_MM_ROWS):
        for r0 in range(m0, m0 + TAIL_MM_ROWS, CONV_ROWS):
            conv_norm_chunk(r0)
        rows = slice(m0, m0 + TAIL_MM_ROWS)
        y_c = jnp.dot(c_ref[rows, :], wc_ref[...], preferred_element_type=jnp.float32)
        a_in = (o_ref[rows, :].astype(jnp.float32)
                * sza_ref[rows, :].astype(jnp.float32)).astype(jnp.bfloat16)
        y_a = jnp.dot(a_in, wa_ref[...], preferred_element_type=jnp.float32)
        merged = (ga_ref[rows, :].astype(jnp.float32) * y_a
                  + gc_ref[rows, :].astype(jnp.float32) * y_c).astype(jnp.bfloat16)
        out_ref[rows, :] = x_ref[rows, :] + jnp.dot(merged, wo_ref[...],
                                                    preferred_element_type=jnp.float32)


def _tail(p_arr, o, x, wa, wc, wo, conv_w, conv_b, cn_g, cn_b, batch, seq):
    m = batch * seq
    ts = TAIL_TILE_M
    n_s = seq // ts

    def p_block(group):
        return pl.BlockSpec((ts, D_MODEL), lambda b, s: (b * n_s + s, group))

    row_block = pl.BlockSpec((ts, D_MODEL), lambda b, s: (b * n_s + s, 0))
    full = lambda shape: pl.BlockSpec(shape, lambda b, s: (0, 0))
    return pl.pallas_call(
        _tail_kernel,
        grid=(batch, n_s),
        in_specs=[
            p_block(P_GLU), p_block(P_SZC), row_block, p_block(P_SZA), p_block(P_GA),
            p_block(P_GC), row_block,
            full((D_MODEL, D_MODEL)), full((D_MODEL, D_MODEL)), full((D_MODEL, D_MODEL)),
            full((CONV_KERNEL, D_MODEL)), full((1, D_MODEL)), full((1, D_MODEL)),
            full((1, D_MODEL)),
        ],
        out_specs=row_block,
        out_shape=jax.ShapeDtypeStruct((m, D_MODEL), jnp.float32),
        scratch_shapes=[
            pltpu.VMEM((D_MODEL // LANES, CONV_HALO + ts, LANES), jnp.float32),
            pltpu.VMEM((ts, D_MODEL), jnp.bfloat16),
        ],
        compiler_params=pltpu.CompilerParams(
            dimension_semantics=("arbitrary", "arbitrary"),
            vmem_limit_bytes=VMEM_LIMIT),
        name="tail",
    )(p_arr, p_arr, o, p_arr, p_arr, p_arr, x, wa, wc, wo, conv_w, conv_b, cn_g, cn_b)


def _rope_tables(seq):
    inv_freq = ROPE_THETA ** (-jnp.arange(0, 2 * ROT_HALF, 2, dtype=jnp.float32) / (2 * ROT_HALF))
    plain = jnp.zeros((_HALF - ROT_HALF,), jnp.float32)
    lane_freq = jnp.concatenate([inv_freq, plain, inv_freq, plain])
    lane_sign = jnp.concatenate([-jnp.ones((_HALF,), jnp.float32), jnp.ones((_HALF,), jnp.float32)])
    ang = jnp.arange(seq, dtype=jnp.float32)[:, None] * lane_freq[None, :]
    return jnp.cos(ang), jnp.sin(ang) * lane_sign[None, :]


def _prepare_w_in(w):
    qk = w[:, :2 * D_MODEL].reshape(D_MODEL, 2 * N_HEADS, HEAD_DIM)
    qk = _permute_head_dims(qk).reshape(D_MODEL, 2 * D_MODEL)
    return qk.astype(jnp.bfloat16), w[:, 2 * D_MODEL:].astype(jnp.bfloat16)


def kernel(x, norm_g, w_in, b_gate, q_norm_g, k_norm_g, conv_w, conv_b, cn_g, cn_b,
           w_attn_proj, w_conv_proj, w_out):
    batch, seq, d = x.shape
    depth = w_in.shape[0]
    assert d == D_MODEL and seq % IN_TILE_M == 0 and seq % TAIL_TILE_M == 0
    cos_t, sin_t = _rope_tables(seq)
    bf = jnp.bfloat16
    xf = x.reshape(batch * seq, d)
    for l in range(depth):
        p_arr, kmean, vt = _in_proj(
            xf, norm_g[l][None], *_prepare_w_in(w_in[l]), b_gate[l][None],
            _permute_head_dims(q_norm_g[l])[None], _permute_head_dims(k_norm_g[l])[None],
            cos_t, sin_t, seq)
        kmean = kmean.reshape(batch, seq // MOBA_BLOCK, D_MODEL)
        score_bound = (LOG2_E * HEAD_DIM ** 0.5 * jnp.max(jnp.abs(q_norm_g[l]))
                       * jnp.max(jnp.abs(k_norm_g[l]))).reshape(1)
        o = _attention(score_bound, p_arr, vt, kmean, batch, seq)
        xf = _tail(p_arr, o, xf, w_attn_proj[l].astype(bf), w_conv_proj[l].astype(bf),
                   w_out[l].astype(bf), conv_w[l], conv_b[l][None], cn_g[l][None],
                   cn_b[l][None], batch, seq)
    return xf.reshape(batch, seq, d)
```

```python
import jax
import jax.numpy as jnp
from jax import lax
from jax.experimental import pallas as pl
from jax.experimental.pallas import tpu as pltpu

D_MODEL = 1024
N_HEADS = 8
HEAD_DIM = 128
ROT_HALF = 16
ROPE_THETA = 500000.0
MOBA_BLOCK = 256
MOBA_TOPK = 3
CONV_KERNEL = 31
CONV_HALO = 32
EPS = 1e-6
MASKED = -1e30
LOG2_E = 1.4426950408889634
ACC_ROWS = HEAD_DIM + 16
FIXED_SHIFT_MAX = 60.0

LANES = 128
IN_TILE_M = 1024
IN_CHUNK_M = MOBA_BLOCK
TAIL_TILE_M = 512
CONV_ROWS = 32
TAIL_MM_ROWS = 256
VMEM_LIMIT = 56 * 1024 * 1024
ATTN_VMEM_LIMIT = 60 * 1024 * 1024

P_Q, P_K, P_SZA, P_GLU, P_SZC, P_GA, P_GC = range(7)
N_P_GROUPS = 7
W_Q, W_K, W_V, W_ZA, W_UA, W_UB, W_ZC, W_GLA, W_GLC = range(9)
N_W_GROUPS = 9

_NT = (((1,), (1,)), ((), ()))

_HALF = HEAD_DIM // 2
_HEAD_PERM_SEGMENTS = ((0, ROT_HALF), (2 * ROT_HALF, _HALF + ROT_HALF),
                       (ROT_HALF, 2 * ROT_HALF), (_HALF + ROT_HALF, HEAD_DIM))


def _permute_head_dims(t):
    return jnp.concatenate([t[..., a:b] for a, b in _HEAD_PERM_SEGMENTS], axis=-1)


def _head_norm_rope(t, g, cos, sin):
    u = t * g
    rotated = u * cos + pltpu.roll(u, _HALF, 1) * sin
    ms = jnp.mean(t * t, axis=-1, keepdims=True)
    return rotated * lax.rsqrt(ms + EPS)


def _in_proj_kernel(x_ref, ng_ref, wqk_ref, w_ref, bg_ref, qg_ref, kg_ref, cos_ref, sin_ref,
                    p_ref, kmean_ref, vt_ref, h_ref, ua_ref, qacc_ref, kacc_ref):
    j = pl.program_id(1)
    tm = x_ref.shape[0]
    head_lanes = [slice(h * HEAD_DIM, (h + 1) * HEAD_DIM) for h in range(N_HEADS)]
    chunk_rows = [slice(c * IN_CHUNK_M, (c + 1) * IN_CHUNK_M) for c in range(tm // IN_CHUNK_M)]

    def row_chunks():
        for c, rows in enumerate(chunk_rows):
            yield c, rows, jnp.dot(h_ref[rows, :], w_ref[...], preferred_element_type=jnp.float32)

    def finish_q():
        scale = LOG2_E / (HEAD_DIM ** 0.5)
        for rows in chunk_rows:
            for hl in head_lanes:
                t = _head_norm_rope(qacc_ref[rows, hl], qg_ref[...], cos_ref[rows, :], sin_ref[rows, :])
                p_ref[rows, hl] = (t * scale).astype(jnp.bfloat16)

    def finish_k():
        for c, rows in enumerate(chunk_rows):
            for hl in head_lanes:
                t = _head_norm_rope(kacc_ref[rows, hl], kg_ref[...], cos_ref[rows, :], sin_ref[rows, :])
                p_ref[rows, hl] = t.astype(jnp.bfloat16)
                kmean_ref[0, c:c + 1, hl] = jnp.mean(t, axis=0, keepdims=True)

    @pl.when(j == W_Q)
    def _():
        x = x_ref[...]
        ms = jnp.mean(x * x, axis=-1, keepdims=True)
        h_ref[...] = (x * lax.rsqrt(ms + EPS) * ng_ref[...]).astype(jnp.bfloat16)
        qacc_ref[...] = jnp.dot(h_ref[...], wqk_ref[...], preferred_element_type=jnp.float32)

    @pl.when(j == W_K)
    def _():
        finish_q()
        kacc_ref[...] = jnp.dot(h_ref[...], wqk_ref[...], preferred_element_type=jnp.float32)

    @pl.when(j == W_V)
    def _():
        finish_k()
        for c, _, acc in row_chunks():
            vt_ref[c] = acc.T.astype(jnp.bfloat16)

    @pl.when((j == W_ZA) | (j == W_ZC))
    def _():
        for _, rows, acc in row_chunks():
            p_ref[rows, :] = (acc * jax.nn.sigmoid(acc)).astype(jnp.bfloat16)

    @pl.when(j == W_UA)
    def _():
        for _, rows, acc in row_chunks():
            ua_ref[rows, :] = acc

    @pl.when(j == W_UB)
    def _():
        for _, rows, acc in row_chunks():
            p_ref[rows, :] = (ua_ref[rows, :] * jax.nn.sigmoid(acc)).astype(jnp.bfloat16)

    @pl.when(j >= W_GLA)
    def _():
        for _, rows, acc in row_chunks():
            p_ref[rows, :] = jax.nn.sigmoid(acc + bg_ref[...]).astype(jnp.bfloat16)


def _in_proj(layer, x, norm_g, w_qk, w_rest, b_gate, q_g, k_g, cos, sin, seq):
    m = x.shape[0]
    tm = IN_TILE_M
    assert IN_CHUNK_M == MOBA_BLOCK
    n_seq_tiles = seq // tm
    blocks_per_tile = tm // MOBA_BLOCK

    def p_col(i, j):
        return (i, j - (j >= W_K).astype(jnp.int32) - (j >= W_UB).astype(jnp.int32))

    return pl.pallas_call(
        _in_proj_kernel,
        grid=(m // tm, N_W_GROUPS),
        in_specs=[
            pl.BlockSpec((tm, D_MODEL), lambda i, j: (i, 0)),
            pl.BlockSpec((1, D_MODEL), lambda i, j: (0, 0)),
            pl.BlockSpec((None, D_MODEL, D_MODEL), lambda i, j: (layer, 0, jnp.minimum(j, W_K))),
            pl.BlockSpec((None, D_MODEL, D_MODEL),
                         lambda i, j: (layer, 0, jnp.maximum(j - W_V, 0))),
            pl.BlockSpec((1, D_MODEL), lambda i, j: (0, jnp.clip(j - W_GLA, 0, 1))),
            pl.BlockSpec((1, HEAD_DIM), lambda i, j: (0, 0)),
            pl.BlockSpec((1, HEAD_DIM), lambda i, j: (0, 0)),
            pl.BlockSpec((tm, HEAD_DIM), lambda i, j: (i % n_seq_tiles, 0)),
            pl.BlockSpec((tm, HEAD_DIM), lambda i, j: (i % n_seq_tiles, 0)),
        ],
        out_specs=[
            pl.BlockSpec((tm, D_MODEL), p_col),
            pl.BlockSpec((1, blocks_per_tile, D_MODEL), lambda i, j: (i, 0, 0)),
            pl.BlockSpec((blocks_per_tile, D_MODEL, MOBA_BLOCK), lambda i, j: (i, 0, 0)),
        ],
        out_shape=[
            jax.ShapeDtypeStruct((m, N_P_GROUPS * D_MODEL), jnp.bfloat16),
            jax.ShapeDtypeStruct((m // tm, blocks_per_tile, D_MODEL), jnp.float32),
            jax.ShapeDtypeStruct((m // MOBA_BLOCK, D_MODEL, MOBA_BLOCK), jnp.bfloat16),
        ],
        scratch_shapes=[
            pltpu.VMEM((tm, D_MODEL), jnp.bfloat16),
            pltpu.VMEM((tm, D_MODEL), jnp.float32),
            pltpu.VMEM((tm, D_MODEL), jnp.float32),
            pltpu.VMEM((tm, D_MODEL), jnp.float32),
        ],
        compiler_params=pltpu.CompilerParams(
            dimension_semantics=("arbitrary", "arbitrary"),
            vmem_limit_bytes=VMEM_LIMIT),
        name="in_proj",
    )(x, norm_g, w_qk, w_rest, b_gate, q_g, k_g, cos, sin)


def _attn_kernel(bound_ref, q_ref, k_ref, vt_ref, km_ref, o_ref, bias_ref, m_ref, acc_ref):
    qi = pl.program_id(1)
    n_blocks = km_ref.shape[1]
    tq = q_ref.shape[0]
    ones_rows = jnp.ones((ACC_ROWS - HEAD_DIM, MOBA_BLOCK), jnp.bfloat16)
    head_lanes = [slice(h * HEAD_DIM, (h + 1) * HEAD_DIM) for h in range(N_HEADS)]
    bound = bound_ref[0]
    fixed_shift = bound <= FIXED_SHIFT_MAX

    def weighted_values(h, j, p):
        v_aug = jnp.concatenate([vt_ref[j, head_lanes[h], :], ones_rows], axis=0)
        return jnp.dot(v_aug, p.astype(jnp.bfloat16), preferred_element_type=jnp.float32)

    def scores(h, start):
        return lax.dot_general(k_ref[pl.ds(start, MOBA_BLOCK), head_lanes[h]],
                               q_ref[:, head_lanes[h]], _NT,
                               preferred_element_type=jnp.float32)

    def for_each_head(start, consume):
        ahead = [scores(0, start), scores(1, start)]
        for h in range(N_HEADS):
            s_t = ahead.pop(0)
            if h + 2 < N_HEADS:
                ahead.append(scores(h + 2, start))
            consume(h, s_t)

    def select_blocks():
        blk_iota = lax.broadcasted_iota(jnp.int32, (n_blocks, tq), 0)
        past = blk_iota < qi
        gates = [lax.dot_general(km_ref[0, :, head_lanes[h]],
                                 q_ref[:, head_lanes[h]].astype(jnp.float32), _NT,
                                 preferred_element_type=jnp.float32) for h in range(N_HEADS)]
        for h in range(N_HEADS):
            g = jnp.where(past, gates[h], -jnp.inf)
            chosen = jnp.zeros((n_blocks, tq), jnp.float32)
            for _ in range(MOBA_TOPK):
                top = jnp.max(g, axis=0, keepdims=True)
                first = jnp.min(jnp.where(g == top, blk_iota, n_blocks), axis=0, keepdims=True)
                pick = blk_iota == first
                chosen = jnp.where(pick, 1.0, chosen)
                g = jnp.where(pick, -jnp.inf, g)
            bias_ref[h] = jnp.where(past, jnp.where(chosen > 0.0, 0.0, MASKED), MASKED)

    key_i = lax.broadcasted_iota(jnp.int32, (MOBA_BLOCK, tq), 0)
    qry_i = lax.broadcasted_iota(jnp.int32, (MOBA_BLOCK, tq), 1)
    own = pl.multiple_of(qi * MOBA_BLOCK, MOBA_BLOCK)

    @pl.when(fixed_shift)
    def _():
        def own_block(h, s_t):
            p = jnp.exp2(jnp.where(key_i <= qry_i, s_t, MASKED) - bound)
            acc_ref[h] = weighted_values(h, qi, p)

        for_each_head(own, own_block)
        select_blocks()

        def probabilities(h, j, s_t):
            shift = bound - bias_ref[h, pl.ds(j, 1), :]
            return jnp.exp2(s_t - shift).astype(jnp.bfloat16)

        def group_step(j0, n):
            start = pl.multiple_of(j0 * MOBA_BLOCK, MOBA_BLOCK)
            ones = jnp.ones((ACC_ROWS - HEAD_DIM, n * MOBA_BLOCK), jnp.bfloat16)

            def group_scores(h):
                return [scores(h, start + b * MOBA_BLOCK) for b in range(n)]

            ahead = [group_scores(0)]
            for h in range(N_HEADS):
                s_blocks = ahead.pop(0)
                if h + 1 < N_HEADS:
                    ahead.append(group_scores(h + 1))
                p = jnp.concatenate([probabilities(h, j0 + b, s_blocks[b]) for b in range(n)],
                                    axis=0)
                v_blocks = jnp.concatenate([vt_ref[j0 + b, head_lanes[h], :] for b in range(n)],
                                           axis=1)
                v_aug = jnp.concatenate([v_blocks, ones], axis=0)
                acc_ref[h] = acc_ref[h] + jnp.dot(v_aug, p, preferred_element_type=jnp.float32)

        def quad_body(i, carry):
            group_step(4 * i, 4)
            return carry

        lax.fori_loop(0, qi // 4, quad_body, 0)
        rest = (qi // 4) * 4

        @pl.when(qi % 4 >= 2)
        def _():
            group_step(rest, 2)

        @pl.when(qi % 2 == 1)
        def _():
            group_step(qi - 1, 1)

    @pl.when(jnp.logical_not(fixed_shift))
    def _():
        def own_block(h, s_t):
            s_t = jnp.where(key_i <= qry_i, s_t, MASKED)
            m0 = jnp.max(s_t, axis=0, keepdims=True)
            m_ref[h] = m0
            acc_ref[h] = weighted_values(h, qi, jnp.exp2(s_t - m0))

        for_each_head(own, own_block)
        select_blocks()

        def body(j, carry):
            def past_block(h, s_t):
                bias = bias_ref[h, pl.ds(j, 1), :]
                m_i = m_ref[h]
                m_new = jnp.maximum(m_i, jnp.max(s_t, axis=0, keepdims=True) + bias)
                alpha = jnp.exp2(m_i - m_new)
                p = jnp.exp2(s_t - (m_new - bias))
                m_ref[h] = m_new
                acc_ref[h] = alpha * acc_ref[h] + weighted_values(h, j, p)

            for_each_head(pl.multiple_of(j * MOBA_BLOCK, MOBA_BLOCK), past_block)
            return carry

        lax.fori_loop(0, qi, body, 0)

    for h in range(N_HEADS):
        acc = acc_ref[h]
        o_t = acc[:HEAD_DIM, :] / acc[HEAD_DIM:HEAD_DIM + 1, :]
        o_ref[:, head_lanes[h]] = o_t.T.astype(o_ref.dtype)


def _attention(score_bound, p_arr, vt, kmean, batch, seq):
    m = batch * seq
    n_blocks = seq // MOBA_BLOCK
    tq = MOBA_BLOCK
    width = N_HEADS * HEAD_DIM
    resident = pl.Buffered(1)
    return pl.pallas_call(
        _attn_kernel,
        grid=(batch, n_blocks),
        in_specs=[
            pl.BlockSpec(memory_space=pltpu.SMEM),
            pl.BlockSpec((tq, width), lambda b, i: (b * n_blocks + i, P_Q)),
            pl.BlockSpec((seq, width), lambda b, i: (b, P_K)),
            pl.BlockSpec((n_blocks, width, MOBA_BLOCK), lambda b, i: (b, 0, 0),
                         pipeline_mode=resident),
            pl.BlockSpec((1, n_blocks, width), lambda b, i: (b, 0, 0)),
        ],
        out_specs=pl.BlockSpec((tq, width), lambda b, i: (b * n_blocks + i, 0)),
        out_shape=jax.ShapeDtypeStruct((m, width), jnp.bfloat16),
        scratch_shapes=[
            pltpu.VMEM((N_HEADS, n_blocks, tq), jnp.float32),
            pltpu.VMEM((N_HEADS, 1, tq), jnp.float32),
            pltpu.VMEM((N_HEADS, ACC_ROWS, tq), jnp.float32),
        ],
        compiler_params=pltpu.CompilerParams(
            dimension_semantics=("arbitrary", "arbitrary"),
            vmem_limit_bytes=ATTN_VMEM_LIMIT),
        name="moba_attn",
    )(score_bound, p_arr, p_arr, vt, kmean)


def _tail_kernel(glu_ref, szc_ref, o_ref, sza_ref, ga_ref, gc_ref, x_ref,
                 wa_ref, wc_ref, wo_ref, cw_ref, cb_ref, cng_ref, cnb_ref,
                 out_ref, hist_ref, c_ref):
    s = pl.program_id(1)
    ts = x_ref.shape[0]
    n_lane_tiles = D_MODEL // LANES
    lane_tiles = [slice(c * LANES, (c + 1) * LANES) for c in range(n_lane_tiles)]

    @pl.when(s == 0)
    def _():
        for c in range(n_lane_tiles):
            hist_ref[c, 0:CONV_HALO, :] = jnp.zeros((CONV_HALO, LANES), jnp.float32)

    @pl.when(s > 0)
    def _():
        for c in range(n_lane_tiles):
            hist_ref[c, 0:CONV_HALO, :] = hist_ref[c, ts:ts + CONV_HALO, :]

    for c in range(n_lane_tiles):
        hist_ref[c, CONV_HALO:, :] = glu_ref[:, lane_tiles[c]].astype(jnp.float32)

    first_tap = CONV_HALO - (CONV_KERNEL - 1)

    def conv_norm_chunk(r0):
        parts = []
        for c in range(n_lane_tiles):
            acc = hist_ref[c, r0 + first_tap:r0 + first_tap + CONV_ROWS, :] * cw_ref[0:1, lane_tiles[c]]
            for t in range(1, CONV_KERNEL):
                acc = acc + (hist_ref[c, r0 + first_tap + t:r0 + first_tap + t + CONV_ROWS, :]
                             * cw_ref[t:t + 1, lane_tiles[c]])
            parts.append(acc)
        conv = jnp.concatenate(parts, axis=1) + cb_ref[...]
        mu = jnp.mean(conv, axis=-1, keepdims=True)
        d = conv - mu
        var = jnp.mean(d * d, axis=-1, keepdims=True)
        y = d * lax.rsqrt(var + EPS) * cng_ref[...] + cnb_ref[...]
        act = y * jax.nn.sigmoid(y)
        gate = szc_ref[r0:r0 + CONV_ROWS, :].astype(jnp.float32)
        c_ref[r0:r0 + CONV_ROWS, :] = (act * gate).astype(jnp.bfloat16)

    for m0 in range(0, ts, TAIL_MM_ROWS):
        for r0 in range(m0, m0 + TAIL_MM_ROWS, CONV_ROWS):
            conv_norm_chunk(r0)
        rows = slice(m0, m0 + TAIL_MM_ROWS)
        y_c = jnp.dot(c_ref[rows, :], wc_ref[...], preferred_element_type=jnp.float32)
        a_in = (o_ref[rows, :].astype(jnp.float32)
                * sza_ref[rows, :].astype(jnp.float32)).astype(jnp.bfloat16)
        y_a = jnp.dot(a_in, wa_ref[...], preferred_element_type=jnp.float32)
        merged = (ga_ref[rows, :].astype(jnp.float32) * y_a
                  + gc_ref[rows, :].astype(jnp.float32) * y_c).astype(jnp.bfloat16)
        out_ref[rows, :] = x_ref[rows, :] + jnp.dot(merged, wo_ref[...],
                                                    preferred_element_type=jnp.float32)


def _tail(layer, p_arr, o, x, wa, wc, wo, conv_w, conv_b, cn_g, cn_b, batch, seq):
    m = batch * seq
    ts = TAIL_TILE_M
    n_s = seq // ts

    def p_block(group):
        return pl.BlockSpec((ts, D_MODEL), lambda b, s: (b * n_s + s, group))

    row_block = pl.BlockSpec((ts, D_MODEL), lambda b, s: (b * n_s + s, 0))
    full = lambda shape: pl.BlockSpec(shape, lambda b, s: (0, 0))
    weight = pl.BlockSpec((None, D_MODEL, D_MODEL), lambda b, s: (layer, 0, 0))
    return pl.pallas_call(
        _tail_kernel,
        grid=(batch, n_s),
        in_specs=[
            p_block(P_GLU), p_block(P_SZC), row_block, p_block(P_SZA), p_block(P_GA),
            p_block(P_GC), row_block,
            weight, weight, weight,
            full((CONV_KERNEL, D_MODEL)), full((1, D_MODEL)), full((1, D_MODEL)),
            full((1, D_MODEL)),
        ],
        out_specs=row_block,
        out_shape=jax.ShapeDtypeStruct((m, D_MODEL), jnp.float32),
        scratch_shapes=[
            pltpu.VMEM((D_MODEL // LANES, CONV_HALO + ts, LANES), jnp.float32),
            pltpu.VMEM((ts, D_MODEL), jnp.bfloat16),
        ],
        compiler_params=pltpu.CompilerParams(
            dimension_semantics=("arbitrary", "arbitrary"),
            vmem_limit_bytes=VMEM_LIMIT),
        name="tail",
    )(p_arr, p_arr, o, p_arr, p_arr, p_arr, x, wa, wc, wo, conv_w, conv_b, cn_g, cn_b)


def _rope_tables(seq):
    inv_freq = ROPE_THETA ** (-jnp.arange(0, 2 * ROT_HALF, 2, dtype=jnp.float32) / (2 * ROT_HALF))
    plain = jnp.zeros((_HALF - ROT_HALF,), jnp.float32)
    lane_freq = jnp.concatenate([inv_freq, plain, inv_freq, plain])
    lane_sign = jnp.concatenate([-jnp.ones((_HALF,), jnp.float32), jnp.ones((_HALF,), jnp.float32)])
    ang = jnp.arange(seq, dtype=jnp.float32)[:, None] * lane_freq[None, :]
    return jnp.cos(ang), jnp.sin(ang) * lane_sign[None, :]


def _prepare_w_in(w):
    depth = w.shape[0]
    qk = w[:, :, :2 * D_MODEL].reshape(depth, D_MODEL, 2 * N_HEADS, HEAD_DIM)
    qk = _permute_head_dims(qk).reshape(depth, D_MODEL, 2 * D_MODEL)
    return qk.astype(jnp.bfloat16), w[:, :, 2 * D_MODEL:].astype(jnp.bfloat16)


def kernel(x, norm_g, w_in, b_gate, q_norm_g, k_norm_g, conv_w, conv_b, cn_g, cn_b,
           w_attn_proj, w_conv_proj, w_out):
    batch, seq, d = x.shape
    depth = w_in.shape[0]
    assert d == D_MODEL and seq % IN_TILE_M == 0 and seq % TAIL_TILE_M == 0
    cos_t, sin_t = _rope_tables(seq)
    bf = jnp.bfloat16
    w_qk, w_rest = _prepare_w_in(w_in)
    w_attn_proj, w_conv_proj, w_out = (w.astype(bf) for w in (w_attn_proj, w_conv_proj, w_out))
    xf = x.reshape(batch * seq, d)
    for l in range(depth):
        p_arr, kmean, vt = _in_proj(
            l, xf, norm_g[l][None], w_qk, w_rest, b_gate[l][None],
            _permute_head_dims(q_norm_g[l])[None], _permute_head_dims(k_norm_g[l])[None],
            cos_t, sin_t, seq)
        kmean = kmean.reshape(batch, seq // MOBA_BLOCK, D_MODEL)
        score_bound = (LOG2_E * HEAD_DIM ** 0.5 * jnp.max(jnp.abs(q_norm_g[l]))
                       * jnp.max(jnp.abs(k_norm_g[l]))).reshape(1)
        o = _attention(score_bound, p_arr, vt, kmean, batch, seq)
        xf = _tail(l, p_arr, o, xf, w_attn_proj, w_conv_proj, w_out, conv_w[l], conv_b[l][None],
                   cn_g[l][None], cn_b[l][None], batch, seq)
    return xf.reshape(batch, seq, d)
```

```python
import jax
import jax.numpy as jnp
from jax import lax
from jax.experimental import pallas as pl
from jax.experimental.pallas import tpu as pltpu

D_MODEL = 1024
N_HEADS = 8
HEAD_DIM = 128
ROT_HALF = 16
ROPE_THETA = 500000.0
MOBA_BLOCK = 256
MOBA_TOPK = 3
CONV_KERNEL = 31
CONV_HALO = 32
EPS = 1e-6
MASKED = -1e30
LOG2_E = 1.4426950408889634
ACC_ROWS = HEAD_DIM + 16
FIXED_SHIFT_MAX = 60.0

LANES = 128
IN_TILE_M = 512
TAIL_TILE_M = 512
CONV_ROWS = 32
TAIL_MM_ROWS = 256
VMEM_LIMIT = 56 * 1024 * 1024
ATTN_VMEM_LIMIT = 60 * 1024 * 1024

P_Q, P_K, P_SZA, P_GLU, P_SZC, P_GA, P_GC = range(7)
N_P_GROUPS = 7
W_Q, W_K, W_V, W_ZA, W_UA, W_UB, W_ZC, W_GLA, W_GLC = range(9)
N_W_GROUPS = 9

_NT = (((1,), (1,)), ((), ()))

_HALF = HEAD_DIM // 2
_HEAD_PERM_SEGMENTS = ((0, ROT_HALF), (2 * ROT_HALF, _HALF + ROT_HALF),
                       (ROT_HALF, 2 * ROT_HALF), (_HALF + ROT_HALF, HEAD_DIM))


def _permute_head_dims(t):
    return jnp.concatenate([t[..., a:b] for a, b in _HEAD_PERM_SEGMENTS], axis=-1)


def _head_norm_rope(t, g, cos, sin):
    u = t * g
    rotated = u * cos + pltpu.roll(u, _HALF, 1) * sin
    ones = jnp.ones((HEAD_DIM, HEAD_DIM), jnp.bfloat16)
    ss = jnp.dot((t * t).astype(jnp.bfloat16), ones, preferred_element_type=jnp.float32)
    return rotated * lax.rsqrt(ss * (1.0 / HEAD_DIM) + EPS)


def _in_proj_kernel(x_ref, ng_ref, wqk_ref, w_ref, bg_ref, qg_ref, kg_ref, cos_ref, sin_ref,
                    p_ref, kmean_ref, vt_ref):
    tm = x_ref.shape[0]
    head_lanes = [slice(h * HEAD_DIM, (h + 1) * HEAD_DIM) for h in range(N_HEADS)]
    group_cols = lambda g: slice(g * D_MODEL, (g + 1) * D_MODEL)

    x = x_ref[...]
    ms = jnp.mean(x * x, axis=-1, keepdims=True)
    h = (x * lax.rsqrt(ms + EPS) * ng_ref[...]).astype(jnp.bfloat16)

    def product(g):
        if g <= W_K:
            w = wqk_ref[:, group_cols(g)]
        else:
            w = w_ref[:, group_cols(g - W_V)]
        return jnp.dot(h, w, preferred_element_type=jnp.float32)

    prod = {}

    def finish_q():
        scale = LOG2_E / (HEAD_DIM ** 0.5)
        for hl in head_lanes:
            t = _head_norm_rope(prod[W_Q][:, hl], qg_ref[...], cos_ref[...], sin_ref[...])
            p_ref[:, hl.start + P_Q * D_MODEL:hl.stop + P_Q * D_MODEL] = (t * scale).astype(jnp.bfloat16)

    def finish_k():
        for hl in head_lanes:
            t = _head_norm_rope(prod[W_K][:, hl], kg_ref[...], cos_ref[...], sin_ref[...])
            p_ref[:, hl.start + P_K * D_MODEL:hl.stop + P_K * D_MODEL] = t.astype(jnp.bfloat16)
            for c in range(tm // MOBA_BLOCK):
                blk = t[c * MOBA_BLOCK:(c + 1) * MOBA_BLOCK, :]
                kmean_ref[0, c:c + 1, hl] = jnp.mean(blk, axis=0, keepdims=True)

    def finish_v():
        for c in range(tm // MOBA_BLOCK):
            vt_ref[c] = prod[W_V][c * MOBA_BLOCK:(c + 1) * MOBA_BLOCK, :].T.astype(jnp.bfloat16)

    def silu(src, dst):
        a = prod[src]
        p_ref[:, group_cols(dst)] = (a * jax.nn.sigmoid(a)).astype(jnp.bfloat16)

    def finish_glu():
        p_ref[:, group_cols(P_GLU)] = (prod[W_UA] * jax.nn.sigmoid(prod[W_UB])).astype(jnp.bfloat16)

    def gate(src, dst, half):
        p_ref[:, group_cols(dst)] = jax.nn.sigmoid(
            prod[src] + bg_ref[:, group_cols(half)]).astype(jnp.bfloat16)

    finish = {
        W_Q: finish_q, W_K: finish_k, W_V: finish_v,
        W_ZA: lambda: silu(W_ZA, P_SZA),
        W_UA: lambda: None,
        W_UB: finish_glu,
        W_ZC: lambda: silu(W_ZC, P_SZC),
        W_GLA: lambda: gate(W_GLA, P_GA, 0),
        W_GLC: lambda: gate(W_GLC, P_GC, 1),
    }
    prod[0] = product(0)
    for g in range(N_W_GROUPS):
        if g + 1 < N_W_GROUPS:
            prod[g + 1] = product(g + 1)
        finish[g]()


def _in_proj(x, norm_g, w_qk, w_rest, b_gate, q_g, k_g, cos, sin, seq):
    m = x.shape[0]
    tm = IN_TILE_M
    assert tm % MOBA_BLOCK == 0
    n_seq_tiles = seq // tm
    blocks_per_tile = tm // MOBA_BLOCK
    resident = pl.Buffered(1)
    const = lambda shape: pl.BlockSpec(shape, lambda i: (0, 0))

    return pl.pallas_call(
        _in_proj_kernel,
        grid=(m // tm,),
        in_specs=[
            pl.BlockSpec((tm, D_MODEL), lambda i: (i, 0)),
            const((1, D_MODEL)),
            pl.BlockSpec(w_qk.shape, lambda i: (0, 0), pipeline_mode=resident),
            pl.BlockSpec(w_rest.shape, lambda i: (0, 0), pipeline_mode=resident),
            const((1, 2 * D_MODEL)),
            const((1, HEAD_DIM)),
            const((1, HEAD_DIM)),
            pl.BlockSpec((tm, HEAD_DIM), lambda i: (i % n_seq_tiles, 0)),
            pl.BlockSpec((tm, HEAD_DIM), lambda i: (i % n_seq_tiles, 0)),
        ],
        out_specs=[
            pl.BlockSpec((tm, N_P_GROUPS * D_MODEL), lambda i: (i, 0)),
            pl.BlockSpec((1, blocks_per_tile, D_MODEL), lambda i: (i, 0, 0)),
            pl.BlockSpec((blocks_per_tile, D_MODEL, MOBA_BLOCK), lambda i: (i, 0, 0)),
        ],
        out_shape=[
            jax.ShapeDtypeStruct((m, N_P_GROUPS * D_MODEL), jnp.bfloat16),
            jax.ShapeDtypeStruct((m // tm, blocks_per_tile, D_MODEL), jnp.float32),
            jax.ShapeDtypeStruct((m // MOBA_BLOCK, D_MODEL, MOBA_BLOCK), jnp.bfloat16),
        ],
        compiler_params=pltpu.CompilerParams(
            dimension_semantics=("arbitrary",),
            vmem_limit_bytes=VMEM_LIMIT),
        name="in_proj",
    )(x, norm_g, w_qk, w_rest, b_gate, q_g, k_g, cos, sin)


def _attn_kernel(bound_ref, q_ref, k_ref, vt_ref, km_ref, o_ref, bias_ref, m_ref, acc_ref):
    qi = pl.program_id(1)
    n_blocks = km_ref.shape[1]
    tq = q_ref.shape[0]
    ones_rows = jnp.ones((ACC_ROWS - HEAD_DIM, MOBA_BLOCK), jnp.bfloat16)
    head_lanes = [slice(h * HEAD_DIM, (h + 1) * HEAD_DIM) for h in range(N_HEADS)]
    bound = bound_ref[0]
    fixed_shift = bound <= FIXED_SHIFT_MAX

    def weighted_values(h, j, p):
        v_aug = jnp.concatenate([vt_ref[j, head_lanes[h], :], ones_rows], axis=0)
        return jnp.dot(v_aug, p.astype(jnp.bfloat16), preferred_element_type=jnp.float32)

    def scores(h, start):
        return lax.dot_general(k_ref[pl.ds(start, MOBA_BLOCK), head_lanes[h]],
                               q_ref[:, head_lanes[h]], _NT,
                               preferred_element_type=jnp.float32)

    def for_each_head(start, consume):
        ahead = [scores(0, start), scores(1, start)]
        for h in range(N_HEADS):
            s_t = ahead.pop(0)
            if h + 2 < N_HEADS:
                ahead.append(scores(h + 2, start))
            consume(h, s_t)

    def select_blocks():
        blk_iota = lax.broadcasted_iota(jnp.int32, (n_blocks, tq), 0)
        past = blk_iota < qi
        gates = [lax.dot_general(km_ref[0, :, head_lanes[h]],
                                 q_ref[:, head_lanes[h]].astype(jnp.float32), _NT,
                                 preferred_element_type=jnp.float32) for h in range(N_HEADS)]
        for h in range(N_HEADS):
            g = jnp.where(past, gates[h], -jnp.inf)
            chosen = jnp.zeros((n_blocks, tq), jnp.float32)
            for _ in range(MOBA_TOPK):
                top = jnp.max(g, axis=0, keepdims=True)
                first = jnp.min(jnp.where(g == top, blk_iota, n_blocks), axis=0, keepdims=True)
                pick = blk_iota == first
                chosen = jnp.where(pick, 1.0, chosen)
                g = jnp.where(pick, -jnp.inf, g)
            bias_ref[h] = jnp.where(past, jnp.where(chosen > 0.0, 0.0, MASKED), MASKED)

    key_i = lax.broadcasted_iota(jnp.int32, (MOBA_BLOCK, tq), 0)
    qry_i = lax.broadcasted_iota(jnp.int32, (MOBA_BLOCK, tq), 1)
    own = pl.multiple_of(qi * MOBA_BLOCK, MOBA_BLOCK)

    @pl.when(fixed_shift)
    def _():
        def own_block(h, s_t):
            p = jnp.exp2(jnp.where(key_i <= qry_i, s_t, MASKED) - bound)
            acc_ref[h] = weighted_values(h, qi, p)

        for_each_head(own, own_block)
        select_blocks()

        def probabilities(h, j, s_t):
            shift = bound - bias_ref[h, pl.ds(j, 1), :]
            return jnp.exp2(s_t - shift).astype(jnp.bfloat16)

        def group_step(j0, n):
            start = pl.multiple_of(j0 * MOBA_BLOCK, MOBA_BLOCK)
            ones = jnp.ones((ACC_ROWS - HEAD_DIM, n * MOBA_BLOCK), jnp.bfloat16)

            def group_scores(h):
                return [scores(h, start + b * MOBA_BLOCK) for b in range(n)]

            ahead = [group_scores(0)]
            for h in range(N_HEADS):
                s_blocks = ahead.pop(0)
                if h + 1 < N_HEADS:
                    ahead.append(group_scores(h + 1))
                p = jnp.concatenate([probabilities(h, j0 + b, s_blocks[b]) for b in range(n)],
                                    axis=0)
                v_blocks = jnp.concatenate([vt_ref[j0 + b, head_lanes[h], :] for b in range(n)],
                                           axis=1)
                v_aug = jnp.concatenate([v_blocks, ones], axis=0)
                acc_ref[h] = acc_ref[h] + jnp.dot(v_aug, p, preferred_element_type=jnp.float32)

        def quad_body(i, carry):
            group_step(4 * i, 4)
            return carry

        lax.fori_loop(0, qi // 4, quad_body, 0)
        rest = (qi // 4) * 4

        @pl.when(qi % 4 >= 2)
        def _():
            group_step(rest, 2)

        @pl.when(qi % 2 == 1)
        def _():
            group_step(qi - 1, 1)

    @pl.when(jnp.logical_not(fixed_shift))
    def _():
        def own_block(h, s_t):
            s_t = jnp.where(key_i <= qry_i, s_t, MASKED)
            m0 = jnp.max(s_t, axis=0, keepdims=True)
            m_ref[h] = m0
            acc_ref[h] = weighted_values(h, qi, jnp.exp2(s_t - m0))

        for_each_head(own, own_block)
        select_blocks()

        def body(j, carry):
            def past_block(h, s_t):
                bias = bias_ref[h, pl.ds(j, 1), :]
                m_i = m_ref[h]
                m_new = jnp.maximum(m_i, jnp.max(s_t, axis=0, keepdims=True) + bias)
                alpha = jnp.exp2(m_i - m_new)
                p = jnp.exp2(s_t - (m_new - bias))
                m_ref[h] = m_new
                acc_ref[h] = alpha * acc_ref[h] + weighted_values(h, j, p)

            for_each_head(pl.multiple_of(j * MOBA_BLOCK, MOBA_BLOCK), past_block)
            return carry

        lax.fori_loop(0, qi, body, 0)

    for h in range(N_HEADS):
        acc = acc_ref[h]
        o_t = acc[:HEAD_DIM, :] / acc[HEAD_DIM:HEAD_DIM + 1, :]
        o_ref[:, head_lanes[h]] = o_t.T.astype(o_ref.dtype)


def _attention(score_bound, p_arr, vt, kmean, batch, seq):
    m = batch * seq
    n_blocks = seq // MOBA_BLOCK
    tq = MOBA_BLOCK
    width = N_HEADS * HEAD_DIM
    resident = pl.Buffered(1)
    return pl.pallas_call(
        _attn_kernel,
        grid=(batch, n_blocks),
        in_specs=[
            pl.BlockSpec(memory_space=pltpu.SMEM),
            pl.BlockSpec((tq, width), lambda b, i: (b * n_blocks + i, P_Q)),
            pl.BlockSpec((seq, width), lambda b, i: (b, P_K)),
            pl.BlockSpec((n_blocks, width, MOBA_BLOCK), lambda b, i: (b, 0, 0),
                         pipeline_mode=resident),
            pl.BlockSpec((1, n_blocks, width), lambda b, i: (b, 0, 0)),
        ],
        out_specs=pl.BlockSpec((tq, width), lambda b, i: (b * n_blocks + i, 0)),
        out_shape=jax.ShapeDtypeStruct((m, width), jnp.bfloat16),
        scratch_shapes=[
            pltpu.VMEM((N_HEADS, n_blocks, tq), jnp.float32),
            pltpu.VMEM((N_HEADS, 1, tq), jnp.float32),
            pltpu.VMEM((N_HEADS, ACC_ROWS, tq), jnp.float32),
        ],
        compiler_params=pltpu.CompilerParams(
            dimension_semantics=("arbitrary", "arbitrary"),
            vmem_limit_bytes=ATTN_VMEM_LIMIT),
        name="moba_attn",
    )(score_bound, p_arr, p_arr, vt, kmean)


def _tail_kernel(glu_ref, szc_ref, o_ref, sza_ref, ga_ref, gc_ref, x_ref,
                 wa_ref, wc_ref, wo_ref, cw_ref, cb_ref, cng_ref, cnb_ref,
                 out_ref, hist_ref, c_ref):
    s = pl.program_id(1)
    ts = x_ref.shape[0]
    n_lane_tiles = D_MODEL // LANES
    lane_tiles = [slice(c * LANES, (c + 1) * LANES) for c in range(n_lane_tiles)]

    @pl.when(s == 0)
    def _():
        for c in range(n_lane_tiles):
            hist_ref[c, 0:CONV_HALO, :] = jnp.zeros((CONV_HALO, LANES), jnp.float32)

    @pl.when(s > 0)
    def _():
        for c in range(n_lane_tiles):
            hist_ref[c, 0:CONV_HALO, :] = hist_ref[c, ts:ts + CONV_HALO, :]

    for c in range(n_lane_tiles):
        hist_ref[c, CONV_HALO:, :] = glu_ref[:, lane_tiles[c]].astype(jnp.float32)

    first_tap = CONV_HALO - (CONV_KERNEL - 1)

    def conv_norm_chunk(r0):
        parts = []
        for c in range(n_lane_tiles):
            acc = hist_ref[c, r0 + first_tap:r0 + first_tap + CONV_ROWS, :] * cw_ref[0:1, lane_tiles[c]]
            for t in range(1, CONV_KERNEL):
                acc = acc + (hist_ref[c, r0 + first_tap + t:r0 + first_tap + t + CONV_ROWS, :]
                             * cw_ref[t:t + 1, lane_tiles[c]])
            parts.append(acc)
        conv = jnp.concatenate(parts, axis=1) + cb_ref[...]
        mu = jnp.mean(conv, axis=-1, keepdims=True)
        d = conv - mu
        var = jnp.mean(d * d, axis=-1, keepdims=True)
        y = d * lax.rsqrt(var + EPS) * cng_ref[...] + cnb_ref[...]
        act = y * jax.nn.sigmoid(y)
        gate = szc_ref[r0:r0 + CONV_ROWS, :].astype(jnp.float32)
        c_ref[r0:r0 + CONV_ROWS, :] = (act * gate).astype(jnp.bfloat16)

    for m0 in range(0, ts, TAIL_MM_ROWS):
        for r0 in range(m0, m0 + TAIL_MM_ROWS, CONV_ROWS):
            conv_norm_chunk(r0)
        rows = slice(m0, m0 + TAIL_MM_ROWS)
        y_c = jnp.dot(c_ref[rows, :], wc_ref[...], preferred_element_type=jnp.float32)
        a_in = (o_ref[rows, :].astype(jnp.float32)
                * sza_ref[rows, :].astype(jnp.float32)).astype(jnp.bfloat16)
        y_a = jnp.dot(a_in, wa_ref[...], preferred_element_type=jnp.float32)
        merged = (ga_ref[rows, :].astype(jnp.float32) * y_a
                  + gc_ref[rows, :].astype(jnp.float32) * y_c).astype(jnp.bfloat16)
        out_ref[rows, :] = x_ref[rows, :] + jnp.dot(merged, wo_ref[...],
                                                    preferred_element_type=jnp.float32)


def _tail(p_arr, o, x, wa, wc, wo, conv_w, conv_b, cn_g, cn_b, batch, seq):
    m = batch * seq
    ts = TAIL_TILE_M
    n_s = seq // ts

    def p_block(group):
        return pl.BlockSpec((ts, D_MODEL), lambda b, s: (b * n_s + s, group))

    row_block = pl.BlockSpec((ts, D_MODEL), lambda b, s: (b * n_s + s, 0))
    full = lambda shape: pl.BlockSpec(shape, lambda b, s: (0, 0))
    return pl.pallas_call(
        _tail_kernel,
        grid=(batch, n_s),
        in_specs=[
            p_block(P_GLU), p_block(P_SZC), row_block, p_block(P_SZA), p_block(P_GA),
            p_block(P_GC), row_block,
            full((D_MODEL, D_MODEL)), full((D_MODEL, D_MODEL)), full((D_MODEL, D_MODEL)),
            full((CONV_KERNEL, D_MODEL)), full((1, D_MODEL)), full((1, D_MODEL)),
            full((1, D_MODEL)),
        ],
        out_specs=row_block,
        out_shape=jax.ShapeDtypeStruct((m, D_MODEL), jnp.float32),
        scratch_shapes=[
            pltpu.VMEM((D_MODEL // LANES, CONV_HALO + ts, LANES), jnp.float32),
            pltpu.VMEM((ts, D_MODEL), jnp.bfloat16),
        ],
        compiler_params=pltpu.CompilerParams(
            dimension_semantics=("arbitrary", "arbitrary"),
            vmem_limit_bytes=VMEM_LIMIT),
        name="tail",
    )(p_arr, p_arr, o, p_arr, p_arr, p_arr, x, wa, wc, wo, conv_w, conv_b, cn_g, cn_b)


def _rope_tables(seq):
    inv_freq = ROPE_THETA ** (-jnp.arange(0, 2 * ROT_HALF, 2, dtype=jnp.float32) / (2 * ROT_HALF))
    plain = jnp.zeros((_HALF - ROT_HALF,), jnp.float32)
    lane_freq = jnp.concatenate([inv_freq, plain, inv_freq, plain])
    lane_sign = jnp.concatenate([-jnp.ones((_HALF,), jnp.float32), jnp.ones((_HALF,), jnp.float32)])
    ang = jnp.arange(seq, dtype=jnp.float32)[:, None] * lane_freq[None, :]
    return jnp.cos(ang), jnp.sin(ang) * lane_sign[None, :]


def _prepare_w_in(w):
    qk = w[:, :2 * D_MODEL].reshape(D_MODEL, 2 * N_HEADS, HEAD_DIM)
    qk = _permute_head_dims(qk).reshape(D_MODEL, 2 * D_MODEL)
    return qk.astype(jnp.bfloat16), w[:, 2 * D_MODEL:].astype(jnp.bfloat16)


def kernel(x, norm_g, w_in, b_gate, q_norm_g, k_norm_g, conv_w, conv_b, cn_g, cn_b,
           w_attn_proj, w_conv_proj, w_out):
    batch, seq, d = x.shape
    depth = w_in.shape[0]
    assert d == D_MODEL and seq % IN_TILE_M == 0 and seq % TAIL_TILE_M == 0
    cos_t, sin_t = _rope_tables(seq)
    bf = jnp.bfloat16
    xf = x.reshape(batch * seq, d)
    for l in range(depth):
        p_arr, kmean, vt = _in_proj(
            xf, norm_g[l][None], *_prepare_w_in(w_in[l]), b_gate[l][None],
            _permute_head_dims(q_norm_g[l])[None], _permute_head_dims(k_norm_g[l])[None],
            cos_t, sin_t, seq)
        kmean = kmean.reshape(batch, seq // MOBA_BLOCK, D_MODEL)
        score_bound = (LOG2_E * HEAD_DIM ** 0.5 * jnp.max(jnp.abs(q_norm_g[l]))
                       * jnp.max(jnp.abs(k_norm_g[l]))).reshape(1)
        o = _attention(score_bound, p_arr, vt, kmean, batch, seq)
        xf = _tail(p_arr, o, xf, w_attn_proj[l].astype(bf), w_conv_proj[l].astype(bf),
                   w_out[l].astype(bf), conv_w[l], conv_b[l][None], cn_g[l][None],
                   cn_b[l][None], batch, seq)
    return xf.reshape(batch, seq, d)
```

```python
import jax
import jax.numpy as jnp
from jax import lax
from jax.experimental import pallas as pl
from jax.experimental.pallas import tpu as pltpu

D_MODEL = 1024
N_HEADS = 8
HEAD_DIM = 128
ROT_HALF = 16
ROPE_THETA = 500000.0
MOBA_BLOCK = 256
MOBA_TOPK = 3
CONV_KERNEL = 31
CONV_HALO = 32
EPS = 1e-6
MASKED = -1e30
LOG2_E = 1.4426950408889634
ACC_ROWS = HEAD_DIM + 16
FIXED_SHIFT_MAX = 60.0
SCORES_AHEAD_OWN = 4
SCORES_AHEAD_PAST = 6

LANES = 128
IN_TILE_M = 512
TAIL_TILE_M = 512
CONV_ROWS = 32
TAIL_MM_ROWS = 256
VMEM_LIMIT = 56 * 1024 * 1024
ATTN_VMEM_LIMIT = 60 * 1024 * 1024

P_Q, P_K, P_SZA, P_GLU, P_SZC, P_GA, P_GC = range(7)
N_P_GROUPS = 7
W_Q, W_K, W_V, W_ZA, W_UA, W_UB, W_ZC, W_GLA, W_GLC = range(9)
N_W_GROUPS = 9

_NT = (((1,), (1,)), ((), ()))

_HALF = HEAD_DIM // 2
_HEAD_PERM_SEGMENTS = ((0, ROT_HALF), (2 * ROT_HALF, _HALF + ROT_HALF),
                       (ROT_HALF, 2 * ROT_HALF), (_HALF + ROT_HALF, HEAD_DIM))


def _permute_head_dims(t):
    return jnp.concatenate([t[..., a:b] for a, b in _HEAD_PERM_SEGMENTS], axis=-1)


def _head_norm_rope(t, g, cos, sin):
    u = t * g
    rotated = u * cos + pltpu.roll(u, _HALF, 1) * sin
    ones = jnp.ones((HEAD_DIM, HEAD_DIM), jnp.bfloat16)
    ss = jnp.dot((t * t).astype(jnp.bfloat16), ones, preferred_element_type=jnp.float32)
    return rotated * lax.rsqrt(ss * (1.0 / HEAD_DIM) + EPS)


def _in_proj_kernel(x_ref, ng_ref, wqk_ref, w_ref, bg_ref, qg_ref, kg_ref, cos_ref, sin_ref,
                    p_ref, kmean_ref, vt_ref):
    tm = x_ref.shape[0]
    head_lanes = [slice(h * HEAD_DIM, (h + 1) * HEAD_DIM) for h in range(N_HEADS)]
    group_cols = lambda g: slice(g * D_MODEL, (g + 1) * D_MODEL)

    x = x_ref[...]
    ms = jnp.mean(x * x, axis=-1, keepdims=True)
    h = (x * lax.rsqrt(ms + EPS) * ng_ref[...]).astype(jnp.bfloat16)

    def product(g):
        if g <= W_K:
            w = wqk_ref[:, group_cols(g)]
        else:
            w = w_ref[:, group_cols(g - W_V)]
        return jnp.dot(h, w, preferred_element_type=jnp.float32)

    prod = {}

    def finish_q():
        scale = LOG2_E / (HEAD_DIM ** 0.5)
        for hl in head_lanes:
            t = _head_norm_rope(prod[W_Q][:, hl], qg_ref[...], cos_ref[...], sin_ref[...])
            p_ref[:, hl.start + P_Q * D_MODEL:hl.stop + P_Q * D_MODEL] = (t * scale).astype(jnp.bfloat16)

    def finish_k():
        for hl in head_lanes:
            t = _head_norm_rope(prod[W_K][:, hl], kg_ref[...], cos_ref[...], sin_ref[...])
            p_ref[:, hl.start + P_K * D_MODEL:hl.stop + P_K * D_MODEL] = t.astype(jnp.bfloat16)
            for c in range(tm // MOBA_BLOCK):
                blk = t[c * MOBA_BLOCK:(c + 1) * MOBA_BLOCK, :]
                kmean_ref[0, c:c + 1, hl] = jnp.mean(blk, axis=0, keepdims=True)

    def finish_v():
        for c in range(tm // MOBA_BLOCK):
            vt_ref[c] = prod[W_V][c * MOBA_BLOCK:(c + 1) * MOBA_BLOCK, :].T.astype(jnp.bfloat16)

    def silu(src, dst):
        a = prod[src]
        p_ref[:, group_cols(dst)] = (a * jax.nn.sigmoid(a)).astype(jnp.bfloat16)

    def finish_glu():
        p_ref[:, group_cols(P_GLU)] = (prod[W_UA] * jax.nn.sigmoid(prod[W_UB])).astype(jnp.bfloat16)

    def gate(src, dst, half):
        p_ref[:, group_cols(dst)] = jax.nn.sigmoid(
            prod[src] + bg_ref[:, group_cols(half)]).astype(jnp.bfloat16)

    finish = {
        W_Q: finish_q, W_K: finish_k, W_V: finish_v,
        W_ZA: lambda: silu(W_ZA, P_SZA),
        W_UA: lambda: None,
        W_UB: finish_glu,
        W_ZC: lambda: silu(W_ZC, P_SZC),
        W_GLA: lambda: gate(W_GLA, P_GA, 0),
        W_GLC: lambda: gate(W_GLC, P_GC, 1),
    }
    prod[0] = product(0)
    for g in range(N_W_GROUPS):
        if g + 1 < N_W_GROUPS:
            prod[g + 1] = product(g + 1)
        finish[g]()


def _in_proj(x, norm_g, w_qk, w_rest, b_gate, q_g, k_g, cos, sin, seq):
    m = x.shape[0]
    tm = IN_TILE_M
    assert tm % MOBA_BLOCK == 0
    n_seq_tiles = seq // tm
    blocks_per_tile = tm // MOBA_BLOCK
    resident = pl.Buffered(1)
    const = lambda shape: pl.BlockSpec(shape, lambda i: (0, 0))

    return pl.pallas_call(
        _in_proj_kernel,
        grid=(m // tm,),
        in_specs=[
            pl.BlockSpec((tm, D_MODEL), lambda i: (i, 0)),
            const((1, D_MODEL)),
            pl.BlockSpec(w_qk.shape, lambda i: (0, 0), pipeline_mode=resident),
            pl.BlockSpec(w_rest.shape, lambda i: (0, 0), pipeline_mode=resident),
            const((1, 2 * D_MODEL)),
            const((1, HEAD_DIM)),
            const((1, HEAD_DIM)),
            pl.BlockSpec((tm, HEAD_DIM), lambda i: (i % n_seq_tiles, 0)),
            pl.BlockSpec((tm, HEAD_DIM), lambda i: (i % n_seq_tiles, 0)),
        ],
        out_specs=[
            pl.BlockSpec((tm, N_P_GROUPS * D_MODEL), lambda i: (i, 0)),
            pl.BlockSpec((1, blocks_per_tile, D_MODEL), lambda i: (i, 0, 0)),
            pl.BlockSpec((blocks_per_tile, D_MODEL, MOBA_BLOCK), lambda i: (i, 0, 0)),
        ],
        out_shape=[
            jax.ShapeDtypeStruct((m, N_P_GROUPS * D_MODEL), jnp.bfloat16),
            jax.ShapeDtypeStruct((m // tm, blocks_per_tile, D_MODEL), jnp.float32),
            jax.ShapeDtypeStruct((m // MOBA_BLOCK, D_MODEL, MOBA_BLOCK), jnp.bfloat16),
        ],
        compiler_params=pltpu.CompilerParams(
            dimension_semantics=("arbitrary",),
            vmem_limit_bytes=VMEM_LIMIT),
        name="in_proj",
    )(x, norm_g, w_qk, w_rest, b_gate, q_g, k_g, cos, sin)


def _attn_kernel(bound_ref, q_ref, k_ref, vt_ref, km_ref, o_ref, bias_ref, m_ref, acc_ref):
    qi = pl.program_id(1)
    n_blocks = km_ref.shape[1]
    tq = q_ref.shape[0]
    ones_rows = jnp.ones((ACC_ROWS - HEAD_DIM, MOBA_BLOCK), jnp.bfloat16)
    head_lanes = [slice(h * HEAD_DIM, (h + 1) * HEAD_DIM) for h in range(N_HEADS)]
    bound = bound_ref[0]
    fixed_shift = bound <= FIXED_SHIFT_MAX

    def weighted_values(h, j, p):
        v_aug = jnp.concatenate([vt_ref[j, head_lanes[h], :], ones_rows], axis=0)
        return jnp.dot(v_aug, p.astype(jnp.bfloat16), preferred_element_type=jnp.float32)

    def scores(h, start):
        return lax.dot_general(k_ref[pl.ds(start, MOBA_BLOCK), head_lanes[h]],
                               q_ref[:, head_lanes[h]], _NT,
                               preferred_element_type=jnp.float32)

    def for_each_head(start, consume):
        ahead = [scores(h, start) for h in range(SCORES_AHEAD_OWN)]
        for h in range(N_HEADS):
            s_t = ahead.pop(0)
            if h + SCORES_AHEAD_OWN < N_HEADS:
                ahead.append(scores(h + SCORES_AHEAD_OWN, start))
            consume(h, s_t)

    def select_blocks():
        blk_iota = lax.broadcasted_iota(jnp.int32, (n_blocks, tq), 0)
        past = blk_iota < qi
        gates = [lax.dot_general(km_ref[0, :, head_lanes[h]],
                                 q_ref[:, head_lanes[h]].astype(jnp.float32), _NT,
                                 preferred_element_type=jnp.float32) for h in range(N_HEADS)]
        for h in range(N_HEADS):
            g = jnp.where(past, gates[h], -jnp.inf)
            chosen = jnp.zeros((n_blocks, tq), jnp.float32)
            for _ in range(MOBA_TOPK):
                top = jnp.max(g, axis=0, keepdims=True)
                first = jnp.min(jnp.where(g == top, blk_iota, n_blocks), axis=0, keepdims=True)
                pick = blk_iota == first
                chosen = jnp.where(pick, 1.0, chosen)
                g = jnp.where(pick, -jnp.inf, g)
            bias_ref[h] = jnp.where(past, jnp.where(chosen > 0.0, 0.0, MASKED), MASKED)

    key_i = lax.broadcasted_iota(jnp.int32, (MOBA_BLOCK, tq), 0)
    qry_i = lax.broadcasted_iota(jnp.int32, (MOBA_BLOCK, tq), 1)
    own = pl.multiple_of(qi * MOBA_BLOCK, MOBA_BLOCK)

    @pl.when(fixed_shift)
    def _():
        def own_block(h, s_t):
            p = jnp.exp2(jnp.where(key_i <= qry_i, s_t, MASKED) - bound)
            acc_ref[h] = weighted_values(h, qi, p)

        for_each_head(own, own_block)
        select_blocks()

        def probabilities(h, j, s_t):
            shift = bound - bias_ref[h, pl.ds(j, 1), :]
            return jnp.exp2(s_t - shift).astype(jnp.bfloat16)

        def group_step(j0, n):
            start = pl.multiple_of(j0 * MOBA_BLOCK, MOBA_BLOCK)
            ones = jnp.ones((ACC_ROWS - HEAD_DIM, n * MOBA_BLOCK), jnp.bfloat16)

            pairs = [(h, b) for h in range(N_HEADS) for b in range(n)]
            look = SCORES_AHEAD_PAST

            def pair_scores(idx):
                h, b = pairs[idx]
                return scores(h, start + b * MOBA_BLOCK)

            ahead = [pair_scores(i) for i in range(min(look, len(pairs)))]
            ps = []
            for idx, (h, b) in enumerate(pairs):
                s_t = ahead.pop(0)
                if idx + look < len(pairs):
                    ahead.append(pair_scores(idx + look))
                ps.append(probabilities(h, j0 + b, s_t))
                if b == n - 1:
                    p = jnp.concatenate(ps, axis=0)
                    ps = []
                    v_blocks = jnp.concatenate([vt_ref[j0 + bb, head_lanes[h], :] for bb in range(n)],
                                               axis=1)
                    v_aug = jnp.concatenate([v_blocks, ones], axis=0)
                    acc_ref[h] = acc_ref[h] + jnp.dot(v_aug, p, preferred_element_type=jnp.float32)

        def quad_body(i, carry):
            group_step(4 * i, 4)
            return carry

        lax.fori_loop(0, qi // 4, quad_body, 0)
        rest = (qi // 4) * 4

        @pl.when(qi % 4 >= 2)
        def _():
            group_step(rest, 2)

        @pl.when(qi % 2 == 1)
        def _():
            group_step(qi - 1, 1)

    @pl.when(jnp.logical_not(fixed_shift))
    def _():
        def own_block(h, s_t):
            s_t = jnp.where(key_i <= qry_i, s_t, MASKED)
            m0 = jnp.max(s_t, axis=0, keepdims=True)
            m_ref[h] = m0
            acc_ref[h] = weighted_values(h, qi, jnp.exp2(s_t - m0))

        for_each_head(own, own_block)
        select_blocks()

        def body(j, carry):
            def past_block(h, s_t):
                bias = bias_ref[h, pl.ds(j, 1), :]
                m_i = m_ref[h]
                m_new = jnp.maximum(m_i, jnp.max(s_t, axis=0, keepdims=True) + bias)
                alpha = jnp.exp2(m_i - m_new)
                p = jnp.exp2(s_t - (m_new - bias))
                m_ref[h] = m_new
                acc_ref[h] = alpha * acc_ref[h] + weighted_values(h, j, p)

            for_each_head(pl.multiple_of(j * MOBA_BLOCK, MOBA_BLOCK), past_block)
            return carry

        lax.fori_loop(0, qi, body, 0)

    for h in range(N_HEADS):
        acc = acc_ref[h]
        o_t = acc[:HEAD_DIM, :] / acc[HEAD_DIM:HEAD_DIM + 1, :]
        o_ref[:, head_lanes[h]] = o_t.T.astype(o_ref.dtype)


def _attention(score_bound, p_arr, vt, kmean, batch, seq):
    m = batch * seq
    n_blocks = seq // MOBA_BLOCK
    tq = MOBA_BLOCK
    width = N_HEADS * HEAD_DIM
    resident = pl.Buffered(1)
    return pl.pallas_call(
        _attn_kernel,
        grid=(batch, n_blocks),
        in_specs=[
            pl.BlockSpec(memory_space=pltpu.SMEM),
            pl.BlockSpec((tq, width), lambda b, i: (b * n_blocks + i, P_Q)),
            pl.BlockSpec((seq, width), lambda b, i: (b, P_K)),
            pl.BlockSpec((n_blocks, width, MOBA_BLOCK), lambda b, i: (b, 0, 0),
                         pipeline_mode=resident),
            pl.BlockSpec((1, n_blocks, width), lambda b, i: (b, 0, 0)),
        ],
        out_specs=pl.BlockSpec((tq, width), lambda b, i: (b * n_blocks + i, 0)),
        out_shape=jax.ShapeDtypeStruct((m, width), jnp.bfloat16),
        scratch_shapes=[
            pltpu.VMEM((N_HEADS, n_blocks, tq), jnp.float32),
            pltpu.VMEM((N_HEADS, 1, tq), jnp.float32),
            pltpu.VMEM((N_HEADS, ACC_ROWS, tq), jnp.float32),
        ],
        compiler_params=pltpu.CompilerParams(
            dimension_semantics=("arbitrary", "arbitrary"),
            vmem_limit_bytes=ATTN_VMEM_LIMIT),
        name="moba_attn",
    )(score_bound, p_arr, p_arr, vt, kmean)


def _tail_kernel(glu_ref, szc_ref, o_ref, sza_ref, ga_ref, gc_ref, x_ref,
                 wa_ref, wc_ref, wo_ref, cw_ref, cb_ref, cng_ref, cnb_ref,
                 out_ref, hist_ref, c_ref):
    s = pl.program_id(1)
    ts = x_ref.shape[0]
    n_lane_tiles = D_MODEL // LANES
    lane_tiles = [slice(c * LANES, (c + 1) * LANES) for c in range(n_lane_tiles)]

    @pl.when(s == 0)
    def _():
        for c in range(n_lane_tiles):
            hist_ref[c, 0:CONV_HALO, :] = jnp.zeros((CONV_HALO, LANES), jnp.float32)

    @pl.when(s > 0)
    def _():
        for c in range(n_lane_tiles):
            hist_ref[c, 0:CONV_HALO, :] = hist_ref[c, ts:ts + CONV_HALO, :]

    for c in range(n_lane_tiles):
        hist_ref[c, CONV_HALO:, :] = glu_ref[:, lane_tiles[c]].astype(jnp.float32)

    first_tap = CONV_HALO - (CONV_KERNEL - 1)

    def conv_norm_chunk(r0):
        parts = []
        for c in range(n_lane_tiles):
            acc = hist_ref[c, r0 + first_tap:r0 + first_tap + CONV_ROWS, :] * cw_ref[0:1, lane_tiles[c]]
            for t in range(1, CONV_KERNEL):
                acc = acc + (hist_ref[c, r0 + first_tap + t:r0 + first_tap + t + CONV_ROWS, :]
                             * cw_ref[t:t + 1, lane_tiles[c]])
            parts.append(acc)
        conv = jnp.concatenate(parts, axis=1) + cb_ref[...]
        mu = jnp.mean(conv, axis=-1, keepdims=True)
        d = conv - mu
        var = jnp.mean(d * d, axis=-1, keepdims=True)
        y = d * lax.rsqrt(var + EPS) * cng_ref[...] + cnb_ref[...]
        act = y * jax.nn.sigmoid(y)
        gate = szc_ref[r0:r0 + CONV_ROWS, :].astype(jnp.float32)
        c_ref[r0:r0 + CONV_ROWS, :] = (act * gate).astype(jnp.bfloat16)

    for m0 in range(0, ts, TAIL_MM_ROWS):
        for r0 in range(m0, m0 + TAIL_MM_ROWS, CONV_ROWS):
            conv_norm_chunk(r0)
        rows = slice(m0, m0 + TAIL_MM_ROWS)
        y_c = jnp.dot(c_ref[rows, :], wc_ref[...], preferred_element_type=jnp.float32)
        a_in = (o_ref[rows, :].astype(jnp.float32)
                * sza_ref[rows, :].astype(jnp.float32)).astype(jnp.bfloat16)
        y_a = jnp.dot(a_in, wa_ref[...], preferred_element_type=jnp.float32)
        merged = (ga_ref[rows, :].astype(jnp.float32) * y_a
                  + gc_ref[rows, :].astype(jnp.float32) * y_c).astype(jnp.bfloat16)
        out_ref[rows, :] = x_ref[rows, :] + jnp.dot(merged, wo_ref[...],
                                                    preferred_element_type=jnp.float32)


def _tail(p_arr, o, x, wa, wc, wo, conv_w, conv_b, cn_g, cn_b, batch, seq):
    m = batch * seq
    ts = TAIL_TILE_M
    n_s = seq // ts

    def p_block(group):
        return pl.BlockSpec((ts, D_MODEL), lambda b, s: (b * n_s + s, group))

    row_block = pl.BlockSpec((ts, D_MODEL), lambda b, s: (b * n_s + s, 0))
    full = lambda shape: pl.BlockSpec(shape, lambda b, s: (0, 0))
    return pl.pallas_call(
        _tail_kernel,
        grid=(batch, n_s),
        in_specs=[
            p_block(P_GLU), p_block(P_SZC), row_block, p_block(P_SZA), p_block(P_GA),
            p_block(P_GC), row_block,
            full((D_MODEL, D_MODEL)), full((D_MODEL, D_MODEL)), full((D_MODEL, D_MODEL)),
            full((CONV_KERNEL, D_MODEL)), full((1, D_MODEL)), full((1, D_MODEL)),
            full((1, D_MODEL)),
        ],
        out_specs=row_block,
        out_shape=jax.ShapeDtypeStruct((m, D_MODEL), jnp.float32),
        scratch_shapes=[
            pltpu.VMEM((D_MODEL // LANES, CONV_HALO + ts, LANES), jnp.float32),
            pltpu.VMEM((ts, D_MODEL), jnp.bfloat16),
        ],
        compiler_params=pltpu.CompilerParams(
            dimension_semantics=("arbitrary", "arbitrary"),
            vmem_limit_bytes=VMEM_LIMIT),
        name="tail",
    )(p_arr, p_arr, o, p_arr, p_arr, p_arr, x, wa, wc, wo, conv_w, conv_b, cn_g, cn_b)


def _rope_tables(seq):
    inv_freq = ROPE_THETA ** (-jnp.arange(0, 2 * ROT_HALF, 2, dtype=jnp.float32) / (2 * ROT_HALF))
    plain = jnp.zeros((_HALF - ROT_HALF,), jnp.float32)
    lane_freq = jnp.concatenate([inv_freq, plain, inv_freq, plain])
    lane_sign = jnp.concatenate([-jnp.ones((_HALF,), jnp.float32), jnp.ones((_HALF,), jnp.float32)])
    ang = jnp.arange(seq, dtype=jnp.float32)[:, None] * lane_freq[None, :]
    return jnp.cos(ang), jnp.sin(ang) * lane_sign[None, :]


def _prepare_w_in(w):
    qk = w[:, :2 * D_MODEL].reshape(D_MODEL, 2 * N_HEADS, HEAD_DIM)
    qk = _permute_head_dims(qk).reshape(D_MODEL, 2 * D_MODEL)
    return qk.astype(jnp.bfloat16), w[:, 2 * D_MODEL:].astype(jnp.bfloat16)


def kernel(x, norm_g, w_in, b_gate, q_norm_g, k_norm_g, conv_w, conv_b, cn_g, cn_b,
           w_attn_proj, w_conv_proj, w_out):
    batch, seq, d = x.shape
    depth = w_in.shape[0]
    assert d == D_MODEL and seq % IN_TILE_M == 0 and seq % TAIL_TILE_M == 0
    cos_t, sin_t = _rope_tables(seq)
    bf = jnp.bfloat16
    xf = x.reshape(batch * seq, d)
    for l in range(depth):
        p_arr, kmean, vt = _in_proj(
            xf, norm_g[l][None], *_prepare_w_in(w_in[l]), b_gate[l][None],
            _permute_head_dims(q_norm_g[l])[None], _permute_head_dims(k_norm_g[l])[None],
            cos_t, sin_t, seq)
        kmean = kmean.reshape(batch, seq // MOBA_BLOCK, D_MODEL)
        score_bound = (LOG2_E * HEAD_DIM ** 0.5 * jnp.max(jnp.abs(q_norm_g[l]))
                       * jnp.max(jnp.abs(k_norm_g[l]))).reshape(1)
        o = _attention(score_bound, p_arr, vt, kmean, batch, seq)
        xf = _tail(p_arr, o, xf, w_attn_proj[l].astype(bf), w_conv_proj[l].astype(bf),
                   w_out[l].astype(bf), conv_w[l], conv_b[l][None], cn_g[l][None],
                   cn_b[l][None], batch, seq)
    return xf.reshape(batch, seq, d)
```

```python
import jax
import jax.numpy as jnp
from jax import lax
from jax.experimental import pallas as pl
from jax.experimental.pallas import tpu as pltpu

D_MODEL = 1024
N_HEADS = 8
HEAD_DIM = 128
ROT_HALF = 16
ROPE_THETA = 500000.0
MOBA_BLOCK = 256
MOBA_TOPK = 3
CONV_KERNEL = 31
CONV_HALO = 32
EPS = 1e-6
MASKED = -1e30
LOG2_E = 1.4426950408889634
ACC_ROWS = HEAD_DIM + 16
FIXED_SHIFT_MAX = 60.0
ATTN_TILE_Q = 512
SCORES_AHEAD_OWN = 4
SCORES_AHEAD_PAST = 6

LANES = 128
IN_TILE_M = 512
TAIL_TILE_M = 1024
CONV_ROWS = 32
TAIL_MM_ROWS = 256
VMEM_LIMIT = 56 * 1024 * 1024
ATTN_VMEM_LIMIT = 60 * 1024 * 1024

P_Q, P_K, P_SZA, P_GLU, P_SZC, P_GA, P_GC = range(7)
N_P_GROUPS = 7
W_Q, W_K, W_V, W_ZA, W_UA, W_UB, W_ZC, W_GLA, W_GLC = range(9)
N_W_GROUPS = 9

_NT = (((1,), (1,)), ((), ()))

_HALF = HEAD_DIM // 2
_HEAD_PERM_SEGMENTS = ((0, ROT_HALF), (2 * ROT_HALF, _HALF + ROT_HALF),
                       (ROT_HALF, 2 * ROT_HALF), (_HALF + ROT_HALF, HEAD_DIM))


def _permute_head_dims(t):
    return jnp.concatenate([t[..., a:b] for a, b in _HEAD_PERM_SEGMENTS], axis=-1)


def _head_norm_rope(t, g, cos, sin):
    u = t * g
    rotated = u * cos + pltpu.roll(u, _HALF, 1) * sin
    ones = jnp.ones((HEAD_DIM, HEAD_DIM), jnp.bfloat16)
    ss = jnp.dot((t * t).astype(jnp.bfloat16), ones, preferred_element_type=jnp.float32)
    return rotated * lax.rsqrt(ss * (1.0 / HEAD_DIM) + EPS)


def _in_proj_kernel(x_ref, ng_ref, wqk_ref, w_ref, bg_ref, qg_ref, kg_ref, cos_ref, sin_ref,
                    p_ref, kmean_ref, vt_ref):
    tm = x_ref.shape[0]
    head_lanes = [slice(h * HEAD_DIM, (h + 1) * HEAD_DIM) for h in range(N_HEADS)]
    group_cols = lambda g: slice(g * D_MODEL, (g + 1) * D_MODEL)

    x = x_ref[...]
    ms = jnp.mean(x * x, axis=-1, keepdims=True)
    h = (x * lax.rsqrt(ms + EPS) * ng_ref[...]).astype(jnp.bfloat16)

    def product(g):
        if g <= W_K:
            w = wqk_ref[:, group_cols(g)]
        else:
            w = w_ref[:, group_cols(g - W_V)]
        return jnp.dot(h, w, preferred_element_type=jnp.float32)

    prod = {}

    def finish_q():
        scale = LOG2_E / (HEAD_DIM ** 0.5)
        for hl in head_lanes:
            t = _head_norm_rope(prod[W_Q][:, hl], qg_ref[...], cos_ref[...], sin_ref[...])
            p_ref[:, hl.start + P_Q * D_MODEL:hl.stop + P_Q * D_MODEL] = (t * scale).astype(jnp.bfloat16)

    def finish_k():
        for hl in head_lanes:
            t = _head_norm_rope(prod[W_K][:, hl], kg_ref[...], cos_ref[...], sin_ref[...])
            p_ref[:, hl.start + P_K * D_MODEL:hl.stop + P_K * D_MODEL] = t.astype(jnp.bfloat16)
            for c in range(tm // MOBA_BLOCK):
                blk = t[c * MOBA_BLOCK:(c + 1) * MOBA_BLOCK, :]
                kmean_ref[0, c:c + 1, hl] = jnp.mean(blk, axis=0, keepdims=True)

    def finish_v():
        for c in range(tm // MOBA_BLOCK):
            vt_ref[c] = prod[W_V][c * MOBA_BLOCK:(c + 1) * MOBA_BLOCK, :].T.astype(jnp.bfloat16)

    def silu(src, dst):
        a = prod[src]
        p_ref[:, group_cols(dst)] = (a * jax.nn.sigmoid(a)).astype(jnp.bfloat16)

    def finish_glu():
        p_ref[:, group_cols(P_GLU)] = (prod[W_UA] * jax.nn.sigmoid(prod[W_UB])).astype(jnp.bfloat16)

    def gate(src, dst, half):
        p_ref[:, group_cols(dst)] = jax.nn.sigmoid(
            prod[src] + bg_ref[:, group_cols(half)]).astype(jnp.bfloat16)

    finish = {
        W_Q: finish_q, W_K: finish_k, W_V: finish_v,
        W_ZA: lambda: silu(W_ZA, P_SZA),
        W_UA: lambda: None,
        W_UB: finish_glu,
        W_ZC: lambda: silu(W_ZC, P_SZC),
        W_GLA: lambda: gate(W_GLA, P_GA, 0),
        W_GLC: lambda: gate(W_GLC, P_GC, 1),
    }
    prod[0] = product(0)
    for g in range(N_W_GROUPS):
        if g + 1 < N_W_GROUPS:
            prod[g + 1] = product(g + 1)
        finish[g]()


def _in_proj(x, norm_g, w_qk, w_rest, b_gate, q_g, k_g, cos, sin, seq):
    m = x.shape[0]
    tm = IN_TILE_M
    assert tm % MOBA_BLOCK == 0
    n_seq_tiles = seq // tm
    blocks_per_tile = tm // MOBA_BLOCK
    resident = pl.Buffered(1)
    const = lambda shape: pl.BlockSpec(shape, lambda i: (0, 0))

    return pl.pallas_call(
        _in_proj_kernel,
        grid=(m // tm,),
        in_specs=[
            pl.BlockSpec((tm, D_MODEL), lambda i: (i, 0)),
            const((1, D_MODEL)),
            pl.BlockSpec(w_qk.shape, lambda i: (0, 0), pipeline_mode=resident),
            pl.BlockSpec(w_rest.shape, lambda i: (0, 0), pipeline_mode=resident),
            const((1, 2 * D_MODEL)),
            const((1, HEAD_DIM)),
            const((1, HEAD_DIM)),
            pl.BlockSpec((tm, HEAD_DIM), lambda i: (i % n_seq_tiles, 0)),
            pl.BlockSpec((tm, HEAD_DIM), lambda i: (i % n_seq_tiles, 0)),
        ],
        out_specs=[
            pl.BlockSpec((tm, N_P_GROUPS * D_MODEL), lambda i: (i, 0)),
            pl.BlockSpec((1, blocks_per_tile, D_MODEL), lambda i: (i, 0, 0)),
            pl.BlockSpec((blocks_per_tile, D_MODEL, MOBA_BLOCK), lambda i: (i, 0, 0)),
        ],
        out_shape=[
            jax.ShapeDtypeStruct((m, N_P_GROUPS * D_MODEL), jnp.bfloat16),
            jax.ShapeDtypeStruct((m // tm, blocks_per_tile, D_MODEL), jnp.float32),
            jax.ShapeDtypeStruct((m // MOBA_BLOCK, D_MODEL, MOBA_BLOCK), jnp.bfloat16),
        ],
        compiler_params=pltpu.CompilerParams(
            dimension_semantics=("arbitrary",),
            vmem_limit_bytes=VMEM_LIMIT),
        name="in_proj",
    )(x, norm_g, w_qk, w_rest, b_gate, q_g, k_g, cos, sin)


def _attn_kernel(bound_ref, q_ref, k_ref, vt_ref, km_ref, o_ref, bias_ref, m_ref, acc_ref):
    n_blocks = km_ref.shape[1]
    tq = MOBA_BLOCK
    lax.fori_loop(0, q_ref.shape[0] // tq,
                  lambda sub, carry: _attn_query_block(
                      sub, bound_ref, q_ref, k_ref, vt_ref, km_ref, o_ref, bias_ref, m_ref, acc_ref),
                  0)


def _attn_query_block(sub, bound_ref, q_ref, k_ref, vt_ref, km_ref, o_ref, bias_ref, m_ref, acc_ref):
    n_blocks = km_ref.shape[1]
    tq = MOBA_BLOCK
    qi = pl.program_id(1) * (q_ref.shape[0] // tq) + sub
    q_rows = pl.ds(pl.multiple_of(sub * tq, tq), tq)
    ones_rows = jnp.ones((ACC_ROWS - HEAD_DIM, MOBA_BLOCK), jnp.bfloat16)
    head_lanes = [slice(h * HEAD_DIM, (h + 1) * HEAD_DIM) for h in range(N_HEADS)]
    bound = bound_ref[0]
    fixed_shift = bound <= FIXED_SHIFT_MAX

    def weighted_values(h, j, p):
        v_aug = jnp.concatenate([vt_ref[j, head_lanes[h], :], ones_rows], axis=0)
        return jnp.dot(v_aug, p.astype(jnp.bfloat16), preferred_element_type=jnp.float32)

    def scores(h, start):
        return lax.dot_general(k_ref[pl.ds(start, MOBA_BLOCK), head_lanes[h]],
                               q_ref[q_rows, head_lanes[h]], _NT,
                               preferred_element_type=jnp.float32)

    def for_each_head(start, consume):
        ahead = [scores(h, start) for h in range(SCORES_AHEAD_OWN)]
        for h in range(N_HEADS):
            s_t = ahead.pop(0)
            if h + SCORES_AHEAD_OWN < N_HEADS:
                ahead.append(scores(h + SCORES_AHEAD_OWN, start))
            consume(h, s_t)

    def select_blocks():
        blk_iota = lax.broadcasted_iota(jnp.int32, (n_blocks, tq), 0)
        past = blk_iota < qi
        gates = [lax.dot_general(km_ref[0, :, head_lanes[h]],
                                 q_ref[q_rows, head_lanes[h]].astype(jnp.float32), _NT,
                                 preferred_element_type=jnp.float32) for h in range(N_HEADS)]
        for h in range(N_HEADS):
            g = jnp.where(past, gates[h], -jnp.inf)
            chosen = jnp.zeros((n_blocks, tq), jnp.float32)
            for _ in range(MOBA_TOPK):
                top = jnp.max(g, axis=0, keepdims=True)
                first = jnp.min(jnp.where(g == top, blk_iota, n_blocks), axis=0, keepdims=True)
                pick = blk_iota == first
                chosen = jnp.where(pick, 1.0, chosen)
                g = jnp.where(pick, -jnp.inf, g)
            bias_ref[h] = jnp.where(past, jnp.where(chosen > 0.0, 0.0, MASKED), MASKED)

    key_i = lax.broadcasted_iota(jnp.int32, (MOBA_BLOCK, tq), 0)
    qry_i = lax.broadcasted_iota(jnp.int32, (MOBA_BLOCK, tq), 1)
    own = pl.multiple_of(qi * MOBA_BLOCK, MOBA_BLOCK)

    @pl.when(fixed_shift)
    def _():
        def own_block(h, s_t):
            p = jnp.exp2(jnp.where(key_i <= qry_i, s_t, MASKED) - bound)
            acc_ref[h] = weighted_values(h, qi, p)

        for_each_head(own, own_block)
        select_blocks()

        def probabilities(h, j, s_t):
            shift = bound - bias_ref[h, pl.ds(j, 1), :]
            return jnp.exp2(s_t - shift).astype(jnp.bfloat16)

        def group_step(j0, n):
            start = pl.multiple_of(j0 * MOBA_BLOCK, MOBA_BLOCK)
            ones = jnp.ones((ACC_ROWS - HEAD_DIM, n * MOBA_BLOCK), jnp.bfloat16)

            pairs = [(h, b) for h in range(N_HEADS) for b in range(n)]
            look = SCORES_AHEAD_PAST

            def pair_scores(idx):
                h, b = pairs[idx]
                return scores(h, start + b * MOBA_BLOCK)

            ahead = [pair_scores(i) for i in range(min(look, len(pairs)))]
            ps = []
            for idx, (h, b) in enumerate(pairs):
                s_t = ahead.pop(0)
                if idx + look < len(pairs):
                    ahead.append(pair_scores(idx + look))
                ps.append(probabilities(h, j0 + b, s_t))
                if b == n - 1:
                    p = jnp.concatenate(ps, axis=0)
                    ps = []
                    v_blocks = jnp.concatenate([vt_ref[j0 + bb, head_lanes[h], :] for bb in range(n)],
                                               axis=1)
                    v_aug = jnp.concatenate([v_blocks, ones], axis=0)
                    acc_ref[h] = acc_ref[h] + jnp.dot(v_aug, p, preferred_element_type=jnp.float32)

        def quad_body(i, carry):
            group_step(4 * i, 4)
            return carry

        lax.fori_loop(0, qi // 4, quad_body, 0)
        rest = (qi // 4) * 4

        @pl.when(qi % 4 >= 2)
        def _():
            group_step(rest, 2)

        @pl.when(qi % 2 == 1)
        def _():
            group_step(qi - 1, 1)

    @pl.when(jnp.logical_not(fixed_shift))
    def _():
        def own_block(h, s_t):
            s_t = jnp.where(key_i <= qry_i, s_t, MASKED)
            m0 = jnp.max(s_t, axis=0, keepdims=True)
            m_ref[h] = m0
            acc_ref[h] = weighted_values(h, qi, jnp.exp2(s_t - m0))

        for_each_head(own, own_block)
        select_blocks()

        def body(j, carry):
            def past_block(h, s_t):
                bias = bias_ref[h, pl.ds(j, 1), :]
                m_i = m_ref[h]
                m_new = jnp.maximum(m_i, jnp.max(s_t, axis=0, keepdims=True) + bias)
                alpha = jnp.exp2(m_i - m_new)
                p = jnp.exp2(s_t - (m_new - bias))
                m_ref[h] = m_new
                acc_ref[h] = alpha * acc_ref[h] + weighted_values(h, j, p)

            for_each_head(pl.multiple_of(j * MOBA_BLOCK, MOBA_BLOCK), past_block)
            return carry

        lax.fori_loop(0, qi, body, 0)

    for h in range(N_HEADS):
        acc = acc_ref[h]
        o_t = acc[:HEAD_DIM, :] / acc[HEAD_DIM:HEAD_DIM + 1, :]
        o_ref[q_rows, head_lanes[h]] = o_t.T.astype(o_ref.dtype)
    return 0


def _attention(score_bound, p_arr, vt, kmean, batch, seq):
    m = batch * seq
    n_blocks = seq // MOBA_BLOCK
    tq = MOBA_BLOCK
    rows = ATTN_TILE_Q
    n_steps = seq // rows
    width = N_HEADS * HEAD_DIM
    resident = pl.Buffered(1)
    return pl.pallas_call(
        _attn_kernel,
        grid=(batch, n_steps),
        in_specs=[
            pl.BlockSpec(memory_space=pltpu.SMEM),
            pl.BlockSpec((rows, width), lambda b, i: (b * n_steps + i, P_Q)),
            pl.BlockSpec((seq, width), lambda b, i: (b, P_K)),
            pl.BlockSpec((n_blocks, width, MOBA_BLOCK), lambda b, i: (b, 0, 0),
                         pipeline_mode=resident),
            pl.BlockSpec((1, n_blocks, width), lambda b, i: (b, 0, 0)),
        ],
        out_specs=pl.BlockSpec((rows, width), lambda b, i: (b * n_steps + i, 0)),
        out_shape=jax.ShapeDtypeStruct((m, width), jnp.bfloat16),
        scratch_shapes=[
            pltpu.VMEM((N_HEADS, n_blocks, tq), jnp.float32),
            pltpu.VMEM((N_HEADS, 1, tq), jnp.float32),
            pltpu.VMEM((N_HEADS, ACC_ROWS, tq), jnp.float32),
        ],
        compiler_params=pltpu.CompilerParams(
            dimension_semantics=("arbitrary", "arbitrary"),
            vmem_limit_bytes=ATTN_VMEM_LIMIT),
        name="moba_attn",
    )(score_bound, p_arr, p_arr, vt, kmean)


def _tail_kernel(glu_ref, szc_ref, o_ref, sza_ref, ga_ref, gc_ref, x_ref,
                 wa_ref, wc_ref, wo_ref, cw_ref, cb_ref, cng_ref, cnb_ref,
                 out_ref, hist_ref, c_ref):
    s = pl.program_id(1)
    ts = x_ref.shape[0]
    n_lane_tiles = D_MODEL // LANES
    lane_tiles = [slice(c * LANES, (c + 1) * LANES) for c in range(n_lane_tiles)]

    @pl.when(s == 0)
    def _():
        for c in range(n_lane_tiles):
            hist_ref[c, 0:CONV_HALO, :] = jnp.zeros((CONV_HALO, LANES), jnp.float32)

    @pl.when(s > 0)
    def _():
        for c in range(n_lane_tiles):
            hist_ref[c, 0:CONV_HALO, :] = hist_ref[c, ts:ts + CONV_HALO, :]

    for c in range(n_lane_tiles):
        hist_ref[c, CONV_HALO:, :] = glu_ref[:, lane_tiles[c]].astype(jnp.float32)

    first_tap = CONV_HALO - (CONV_KERNEL - 1)

    def conv_norm_chunk(r0):
        parts = []
        for c in range(n_lane_tiles):
            acc = hist_ref[c, r0 + first_tap:r0 + first_tap + CONV_ROWS, :] * cw_ref[0:1, lane_tiles[c]]
            for t in range(1, CONV_KERNEL):
                acc = acc + (hist_ref[c, r0 + first_tap + t:r0 + first_tap + t + CONV_ROWS, :]
                             * cw_ref[t:t + 1, lane_tiles[c]])
            parts.append(acc)
        conv = jnp.concatenate(parts, axis=1) + cb_ref[...]
        mu = jnp.mean(conv, axis=-1, keepdims=True)
        d = conv - mu
        var = jnp.mean(d * d, axis=-1, keepdims=True)
        y = d * lax.rsqrt(var + EPS) * cng_ref[...] + cnb_ref[...]
        act = y * jax.nn.sigmoid(y)
        gate = szc_ref[r0:r0 + CONV_ROWS, :].astype(jnp.float32)
        c_ref[r0:r0 + CONV_ROWS, :] = (act * gate).astype(jnp.bfloat16)

    for m0 in range(0, ts, TAIL_MM_ROWS):
        for r0 in range(m0, m0 + TAIL_MM_ROWS, CONV_ROWS):
            conv_norm_chunk(r0)
        rows = slice(m0, m0 + TAIL_MM_ROWS)
        y_c = jnp.dot(c_ref[rows, :], wc_ref[...], preferred_element_type=jnp.float32)
        a_in = (o_ref[rows, :].astype(jnp.float32)
                * sza_ref[rows, :].astype(jnp.float32)).astype(jnp.bfloat16)
        y_a = jnp.dot(a_in, wa_ref[...], preferred_element_type=jnp.float32)
        merged = (ga_ref[rows, :].astype(jnp.float32) * y_a
                  + gc_ref[rows, :].astype(jnp.float32) * y_c).astype(jnp.bfloat16)
        out_ref[rows, :] = x_ref[rows, :] + jnp.dot(merged, wo_ref[...],
                                                    preferred_element_type=jnp.float32)


def _tail(p_arr, o, x, wa, wc, wo, conv_w, conv_b, cn_g, cn_b, batch, seq):
    m = batch * seq
    ts = TAIL_TILE_M
    n_s = seq // ts

    def p_block(group):
        return pl.BlockSpec((ts, D_MODEL), lambda b, s: (b * n_s + s, group))

    row_block = pl.BlockSpec((ts, D_MODEL), lambda b, s: (b * n_s + s, 0))
    full = lambda shape: pl.BlockSpec(shape, lambda b, s: (0, 0))
    weight = pl.BlockSpec((D_MODEL, D_MODEL), lambda b, s: (0, 0), pipeline_mode=pl.Buffered(1))
    return pl.pallas_call(
        _tail_kernel,
        grid=(batch, n_s),
        in_specs=[
            p_block(P_GLU), p_block(P_SZC), row_block, p_block(P_SZA), p_block(P_GA),
            p_block(P_GC), row_block,
            weight, weight, weight,
            full((CONV_KERNEL, D_MODEL)), full((1, D_MODEL)), full((1, D_MODEL)),
            full((1, D_MODEL)),
        ],
        out_specs=row_block,
        out_shape=jax.ShapeDtypeStruct((m, D_MODEL), jnp.float32),
        scratch_shapes=[
            pltpu.VMEM((D_MODEL // LANES, CONV_HALO + ts, LANES), jnp.float32),
            pltpu.VMEM((ts, D_MODEL), jnp.bfloat16),
        ],
        compiler_params=pltpu.CompilerParams(
            dimension_semantics=("arbitrary", "arbitrary"),
            vmem_limit_bytes=VMEM_LIMIT),
        name="tail",
    )(p_arr, p_arr, o, p_arr, p_arr, p_arr, x, wa, wc, wo, conv_w, conv_b, cn_g, cn_b)


def _rope_tables(seq):
    inv_freq = ROPE_THETA ** (-jnp.arange(0, 2 * ROT_HALF, 2, dtype=jnp.float32) / (2 * ROT_HALF))
    plain = jnp.zeros((_HALF - ROT_HALF,), jnp.float32)
    lane_freq = jnp.concatenate([inv_freq, plain, inv_freq, plain])
    lane_sign = jnp.concatenate([-jnp.ones((_HALF,), jnp.float32), jnp.ones((_HALF,), jnp.float32)])
    ang = jnp.arange(seq, dtype=jnp.float32)[:, None] * lane_freq[None, :]
    return jnp.cos(ang), jnp.sin(ang) * lane_sign[None, :]


def _prepare_w_in(w):
    qk = w[:, :2 * D_MODEL].reshape(D_MODEL, 2 * N_HEADS, HEAD_DIM)
    qk = _permute_head_dims(qk).reshape(D_MODEL, 2 * D_MODEL)
    return qk.astype(jnp.bfloat16), w[:, 2 * D_MODEL:].astype(jnp.bfloat16)


def kernel(x, norm_g, w_in, b_gate, q_norm_g, k_norm_g, conv_w, conv_b, cn_g, cn_b,
           w_attn_proj, w_conv_proj, w_out):
    batch, seq, d = x.shape
    depth = w_in.shape[0]
    assert d == D_MODEL and seq % IN_TILE_M == 0 and seq % TAIL_TILE_M == 0
    cos_t, sin_t = _rope_tables(seq)
    bf = jnp.bfloat16
    xf = x.reshape(batch * seq, d)
    for l in range(depth):
        p_arr, kmean, vt = _in_proj(
            xf, norm_g[l][None], *_prepare_w_in(w_in[l]), b_gate[l][None],
            _permute_head_dims(q_norm_g[l])[None], _permute_head_dims(k_norm_g[l])[None],
            cos_t, sin_t, seq)
        kmean = kmean.reshape(batch, seq // MOBA_BLOCK, D_MODEL)
        score_bound = (LOG2_E * HEAD_DIM ** 0.5 * jnp.max(jnp.abs(q_norm_g[l]))
                       * jnp.max(jnp.abs(k_norm_g[l]))).reshape(1)
        o = _attention(score_bound, p_arr, vt, kmean, batch, seq)
        xf = _tail(p_arr, o, xf, w_attn_proj[l].astype(bf), w_conv_proj[l].astype(bf),
                   w_out[l].astype(bf), conv_w[l], conv_b[l][None], cn_g[l][None],
                   cn_b[l][None], batch, seq)
    return xf.reshape(batch, seq, d)
```

```python
import jax
import jax.numpy as jnp
from jax import lax
from jax.experimental import pallas as pl
from jax.experimental.pallas import tpu as pltpu

D_MODEL = 1024
N_HEADS = 8
HEAD_DIM = 128
ROT_HALF = 16
ROPE_THETA = 500000.0
MOBA_BLOCK = 256
MOBA_TOPK = 3
CONV_KERNEL = 31
CONV_HALO = 32
EPS = 1e-6
MASKED = -1e30
LOG2_E = 1.4426950408889634
ACC_ROWS = HEAD_DIM + 16
FIXED_SHIFT_MAX = 60.0
SCORES_AHEAD_OWN = 4
SCORES_AHEAD_PAST = 6

LANES = 128
IN_TILE_M = 512
TAIL_TILE_M = 512
CONV_ROWS = 32
TAIL_MM_ROWS = 256
VMEM_LIMIT = 56 * 1024 * 1024
ATTN_VMEM_LIMIT = 60 * 1024 * 1024

P_Q, P_K, P_SZA, P_GLU, P_SZC, P_GA, P_GC = range(7)
N_P_GROUPS = 7
W_Q, W_K, W_V, W_ZA, W_UA, W_UB, W_ZC, W_GLA, W_GLC = range(9)
N_W_GROUPS = 9

_NT = (((1,), (1,)), ((), ()))

_HALF = HEAD_DIM // 2
_HEAD_PERM_SEGMENTS = ((0, ROT_HALF), (2 * ROT_HALF, _HALF + ROT_HALF),
                       (ROT_HALF, 2 * ROT_HALF), (_HALF + ROT_HALF, HEAD_DIM))


def _permute_head_dims(t):
    return jnp.concatenate([t[..., a:b] for a, b in _HEAD_PERM_SEGMENTS], axis=-1)


def _head_norm_rope(t, g, cos, sin):
    u = t * g
    rotated = u * cos + pltpu.roll(u, _HALF, 1) * sin
    ones = jnp.ones((HEAD_DIM, HEAD_DIM), jnp.bfloat16)
    ss = jnp.dot((t * t).astype(jnp.bfloat16), ones, preferred_element_type=jnp.float32)
    return rotated * lax.rsqrt(ss * (1.0 / HEAD_DIM) + EPS)


def _in_proj_kernel(x_ref, ng_ref, wqk_ref, w_ref, bg_ref, qg_ref, kg_ref, cos_ref, sin_ref,
                    p_ref, kmean_ref, vt_ref):
    tm = x_ref.shape[0]
    head_lanes = [slice(h * HEAD_DIM, (h + 1) * HEAD_DIM) for h in range(N_HEADS)]
    group_cols = lambda g: slice(g * D_MODEL, (g + 1) * D_MODEL)

    x = x_ref[...]
    ms = jnp.mean(x * x, axis=-1, keepdims=True)
    h = (x * lax.rsqrt(ms + EPS) * ng_ref[...]).astype(jnp.bfloat16)

    def product(g):
        if g <= W_K:
            w = wqk_ref[:, group_cols(g)]
        else:
            w = w_ref[:, group_cols(g - W_V)]
        return jnp.dot(h, w, preferred_element_type=jnp.float32)

    prod = {}

    def finish_q():
        scale = LOG2_E / (HEAD_DIM ** 0.5)
        for hl in head_lanes:
            t = _head_norm_rope(prod[W_Q][:, hl], qg_ref[...], cos_ref[...], sin_ref[...])
            p_ref[:, hl.start + P_Q * D_MODEL:hl.stop + P_Q * D_MODEL] = (t * scale).astype(jnp.bfloat16)

    def finish_k():
        for hl in head_lanes:
            t = _head_norm_rope(prod[W_K][:, hl], kg_ref[...], cos_ref[...], sin_ref[...])
            p_ref[:, hl.start + P_K * D_MODEL:hl.stop + P_K * D_MODEL] = t.astype(jnp.bfloat16)
            for c in range(tm // MOBA_BLOCK):
                blk = t[c * MOBA_BLOCK:(c + 1) * MOBA_BLOCK, :]
                kmean_ref[0, c:c + 1, hl] = jnp.mean(blk, axis=0, keepdims=True)

    def finish_v():
        for c in range(tm // MOBA_BLOCK):
            vt_ref[c] = prod[W_V][c * MOBA_BLOCK:(c + 1) * MOBA_BLOCK, :].T.astype(jnp.bfloat16)

    def silu(src, dst):
        a = prod[src]
        p_ref[:, group_cols(dst)] = (a * jax.nn.sigmoid(a)).astype(jnp.bfloat16)

    def finish_glu():
        p_ref[:, group_cols(P_GLU)] = (prod[W_UA] * jax.nn.sigmoid(prod[W_UB])).astype(jnp.bfloat16)

    def gate(src, dst, half):
        p_ref[:, group_cols(dst)] = jax.nn.sigmoid(
            prod[src] + bg_ref[:, group_cols(half)]).astype(jnp.bfloat16)

    finish = {
        W_Q: finish_q, W_K: finish_k, W_V: finish_v,
        W_ZA: lambda: silu(W_ZA, P_SZA),
        W_UA: lambda: None,
        W_UB: finish_glu,
        W_ZC: lambda: silu(W_ZC, P_SZC),
        W_GLA: lambda: gate(W_GLA, P_GA, 0),
        W_GLC: lambda: gate(W_GLC, P_GC, 1),
    }
    prod[0] = product(0)
    for g in range(N_W_GROUPS):
        if g + 1 < N_W_GROUPS:
            prod[g + 1] = product(g + 1)
        finish[g]()


def _in_proj(layer, x, norm_g, w_qk, w_rest, b_gate, q_g, k_g, cos, sin, seq):
    m = x.shape[0]
    tm = IN_TILE_M
    assert tm % MOBA_BLOCK == 0
    n_seq_tiles = seq // tm
    blocks_per_tile = tm // MOBA_BLOCK
    resident = pl.Buffered(1)
    const = lambda shape: pl.BlockSpec(shape, lambda i: (0, 0))

    return pl.pallas_call(
        _in_proj_kernel,
        grid=(m // tm,),
        in_specs=[
            pl.BlockSpec((tm, D_MODEL), lambda i: (i, 0)),
            const((1, D_MODEL)),
            pl.BlockSpec((None,) + w_qk.shape[1:], lambda i: (layer, 0, 0), pipeline_mode=resident),
            pl.BlockSpec((None,) + w_rest.shape[1:], lambda i: (layer, 0, 0), pipeline_mode=resident),
            const((1, 2 * D_MODEL)),
            const((1, HEAD_DIM)),
            const((1, HEAD_DIM)),
            pl.BlockSpec((tm, HEAD_DIM), lambda i: (i % n_seq_tiles, 0)),
            pl.BlockSpec((tm, HEAD_DIM), lambda i: (i % n_seq_tiles, 0)),
        ],
        out_specs=[
            pl.BlockSpec((tm, N_P_GROUPS * D_MODEL), lambda i: (i, 0)),
            pl.BlockSpec((1, blocks_per_tile, D_MODEL), lambda i: (i, 0, 0)),
            pl.BlockSpec((blocks_per_tile, D_MODEL, MOBA_BLOCK), lambda i: (i, 0, 0)),
        ],
        out_shape=[
            jax.ShapeDtypeStruct((m, N_P_GROUPS * D_MODEL), jnp.bfloat16),
            jax.ShapeDtypeStruct((m // tm, blocks_per_tile, D_MODEL), jnp.float32),
            jax.ShapeDtypeStruct((m // MOBA_BLOCK, D_MODEL, MOBA_BLOCK), jnp.bfloat16),
        ],
        compiler_params=pltpu.CompilerParams(
            dimension_semantics=("arbitrary",),
            vmem_limit_bytes=VMEM_LIMIT),
        name="in_proj",
    )(x, norm_g, w_qk, w_rest, b_gate, q_g, k_g, cos, sin)


def _attn_kernel(bound_ref, q_ref, k_ref, vt_ref, km_ref, o_ref, bias_ref, m_ref, acc_ref):
    qi = pl.program_id(1)
    n_blocks = km_ref.shape[1]
    tq = q_ref.shape[0]
    ones_rows = jnp.ones((ACC_ROWS - HEAD_DIM, MOBA_BLOCK), jnp.bfloat16)
    head_lanes = [slice(h * HEAD_DIM, (h + 1) * HEAD_DIM) for h in range(N_HEADS)]
    bound = bound_ref[0]
    fixed_shift = bound <= FIXED_SHIFT_MAX

    def weighted_values(h, j, p):
        v_aug = jnp.concatenate([vt_ref[j, head_lanes[h], :], ones_rows], axis=0)
        return jnp.dot(v_aug, p.astype(jnp.bfloat16), preferred_element_type=jnp.float32)

    def scores(h, start):
        return lax.dot_general(k_ref[pl.ds(start, MOBA_BLOCK), head_lanes[h]],
                               q_ref[:, head_lanes[h]], _NT,
                               preferred_element_type=jnp.float32)

    def for_each_head(start, consume):
        ahead = [scores(h, start) for h in range(SCORES_AHEAD_OWN)]
        for h in range(N_HEADS):
            s_t = ahead.pop(0)
            if h + SCORES_AHEAD_OWN < N_HEADS:
                ahead.append(scores(h + SCORES_AHEAD_OWN, start))
            consume(h, s_t)

    def select_blocks():
        blk_iota = lax.broadcasted_iota(jnp.int32, (n_blocks, tq), 0)
        past = blk_iota < qi
        gates = [lax.dot_general(km_ref[0, :, head_lanes[h]],
                                 q_ref[:, head_lanes[h]].astype(jnp.float32), _NT,
                                 preferred_element_type=jnp.float32) for h in range(N_HEADS)]
        for h in range(N_HEADS):
            g = jnp.where(past, gates[h], -jnp.inf)
            chosen = jnp.zeros((n_blocks, tq), jnp.float32)
            for _ in range(MOBA_TOPK):
                top = jnp.max(g, axis=0, keepdims=True)
                first = jnp.min(jnp.where(g == top, blk_iota, n_blocks), axis=0, keepdims=True)
                pick = blk_iota == first
                chosen = jnp.where(pick, 1.0, chosen)
                g = jnp.where(pick, -jnp.inf, g)
            bias_ref[h] = jnp.where(past, jnp.where(chosen > 0.0, 0.0, MASKED), MASKED)

    key_i = lax.broadcasted_iota(jnp.int32, (MOBA_BLOCK, tq), 0)
    qry_i = lax.broadcasted_iota(jnp.int32, (MOBA_BLOCK, tq), 1)
    own = pl.multiple_of(qi * MOBA_BLOCK, MOBA_BLOCK)

    @pl.when(fixed_shift)
    def _():
        def own_block(h, s_t):
            p = jnp.exp2(jnp.where(key_i <= qry_i, s_t, MASKED) - bound)
            acc_ref[h] = weighted_values(h, qi, p)

        for_each_head(own, own_block)
        select_blocks()

        def probabilities(h, j, s_t):
            shift = bound - bias_ref[h, pl.ds(j, 1), :]
            return jnp.exp2(s_t - shift).astype(jnp.bfloat16)

        def group_step(j0, n):
            start = pl.multiple_of(j0 * MOBA_BLOCK, MOBA_BLOCK)
            ones = jnp.ones((ACC_ROWS - HEAD_DIM, n * MOBA_BLOCK), jnp.bfloat16)

            pairs = [(h, b) for h in range(N_HEADS) for b in range(n)]
            look = SCORES_AHEAD_PAST

            def pair_scores(idx):
                h, b = pairs[idx]
                return scores(h, start + b * MOBA_BLOCK)

            ahead = [pair_scores(i) for i in range(min(look, len(pairs)))]
            ps = []
            for idx, (h, b) in enumerate(pairs):
                s_t = ahead.pop(0)
                if idx + look < len(pairs):
                    ahead.append(pair_scores(idx + look))
                ps.append(probabilities(h, j0 + b, s_t))
                if b == n - 1:
                    p = jnp.concatenate(ps, axis=0)
                    ps = []
                    v_blocks = jnp.concatenate([vt_ref[j0 + bb, head_lanes[h], :] for bb in range(n)],
                                               axis=1)
                    v_aug = jnp.concatenate([v_blocks, ones], axis=0)
                    acc_ref[h] = acc_ref[h] + jnp.dot(v_aug, p, preferred_element_type=jnp.float32)

        def quad_body(i, carry):
            group_step(4 * i, 4)
            return carry

        lax.fori_loop(0, qi // 4, quad_body, 0)
        rest = (qi // 4) * 4

        @pl.when(qi % 4 >= 2)
        def _():
            group_step(rest, 2)

        @pl.when(qi % 2 == 1)
        def _():
            group_step(qi - 1, 1)

    @pl.when(jnp.logical_not(fixed_shift))
    def _():
        def own_block(h, s_t):
            s_t = jnp.where(key_i <= qry_i, s_t, MASKED)
            m0 = jnp.max(s_t, axis=0, keepdims=True)
            m_ref[h] = m0
            acc_ref[h] = weighted_values(h, qi, jnp.exp2(s_t - m0))

        for_each_head(own, own_block)
        select_blocks()

        def body(j, carry):
            def past_block(h, s_t):
                bias = bias_ref[h, pl.ds(j, 1), :]
                m_i = m_ref[h]
                m_new = jnp.maximum(m_i, jnp.max(s_t, axis=0, keepdims=True) + bias)
                alpha = jnp.exp2(m_i - m_new)
                p = jnp.exp2(s_t - (m_new - bias))
                m_ref[h] = m_new
                acc_ref[h] = alpha * acc_ref[h] + weighted_values(h, j, p)

            for_each_head(pl.multiple_of(j * MOBA_BLOCK, MOBA_BLOCK), past_block)
            return carry

        lax.fori_loop(0, qi, body, 0)

    for h in range(N_HEADS):
        acc = acc_ref[h]
        o_t = acc[:HEAD_DIM, :] / acc[HEAD_DIM:HEAD_DIM + 1, :]
        o_ref[:, head_lanes[h]] = o_t.T.astype(o_ref.dtype)


def _attention(score_bound, p_arr, vt, kmean, batch, seq):
    m = batch * seq
    n_blocks = seq // MOBA_BLOCK
    tq = MOBA_BLOCK
    width = N_HEADS * HEAD_DIM
    resident = pl.Buffered(1)
    return pl.pallas_call(
        _attn_kernel,
        grid=(batch, n_blocks),
        in_specs=[
            pl.BlockSpec(memory_space=pltpu.SMEM),
            pl.BlockSpec((tq, width), lambda b, i: (b * n_blocks + i, P_Q)),
            pl.BlockSpec((seq, width), lambda b, i: (b, P_K)),
            pl.BlockSpec((n_blocks, width, MOBA_BLOCK), lambda b, i: (b, 0, 0),
                         pipeline_mode=resident),
            pl.BlockSpec((1, n_blocks, width), lambda b, i: (b, 0, 0)),
        ],
        out_specs=pl.BlockSpec((tq, width), lambda b, i: (b * n_blocks + i, 0)),
        out_shape=jax.ShapeDtypeStruct((m, width), jnp.bfloat16),
        scratch_shapes=[
            pltpu.VMEM((N_HEADS, n_blocks, tq), jnp.float32),
            pltpu.VMEM((N_HEADS, 1, tq), jnp.float32),
            pltpu.VMEM((N_HEADS, ACC_ROWS, tq), jnp.float32),
        ],
        compiler_params=pltpu.CompilerParams(
            dimension_semantics=("arbitrary", "arbitrary"),
            vmem_limit_bytes=ATTN_VMEM_LIMIT),
        name="moba_attn",
    )(score_bound, p_arr, p_arr, vt, kmean)


def _tail_kernel(glu_ref, szc_ref, o_ref, sza_ref, ga_ref, gc_ref, x_ref,
                 wa_ref, wc_ref, wo_ref, cw_ref, cb_ref, cng_ref, cnb_ref,
                 out_ref, hist_ref, c_ref):
    s = pl.program_id(1)
    ts = x_ref.shape[0]
    n_lane_tiles = D_MODEL // LANES
    lane_tiles = [slice(c * LANES, (c + 1) * LANES) for c in range(n_lane_tiles)]

    @pl.when(s == 0)
    def _():
        for c in range(n_lane_tiles):
            hist_ref[c, 0:CONV_HALO, :] = jnp.zeros((CONV_HALO, LANES), jnp.float32)

    @pl.when(s > 0)
    def _():
        for c in range(n_lane_tiles):
            hist_ref[c, 0:CONV_HALO, :] = hist_ref[c, ts:ts + CONV_HALO, :]

    for c in range(n_lane_tiles):
        hist_ref[c, CONV_HALO:, :] = glu_ref[:, lane_tiles[c]].astype(jnp.float32)

    first_tap = CONV_HALO - (CONV_KERNEL - 1)

    def conv_norm_chunk(r0):
        parts = []
        for c in range(n_lane_tiles):
            acc = hist_ref[c, r0 + first_tap:r0 + first_tap + CONV_ROWS, :] * cw_ref[0:1, lane_tiles[c]]
            for t in range(1, CONV_KERNEL):
                acc = acc + (hist_ref[c, r0 + first_tap + t:r0 + first_tap + t + CONV_ROWS, :]
                             * cw_ref[t:t + 1, lane_tiles[c]])
            parts.append(acc)
        conv = jnp.concatenate(parts, axis=1) + cb_ref[...]
        mu = jnp.mean(conv, axis=-1, keepdims=True)
        d = conv - mu
        var = jnp.mean(d * d, axis=-1, keepdims=True)
        y = d * lax.rsqrt(var + EPS) * cng_ref[...] + cnb_ref[...]
        act = y * jax.nn.sigmoid(y)
        gate = szc_ref[r0:r0 + CONV_ROWS, :].astype(jnp.float32)
        c_ref[r0:r0 + CONV_ROWS, :] = (act * gate).astype(jnp.bfloat16)

    for m0 in range(0, ts, TAIL_MM_ROWS):
        for r0 in range(m0, m0 + TAIL_MM_ROWS, CONV_ROWS):
            conv_norm_chunk(r0)
        rows = slice(m0, m0 + TAIL_MM_ROWS)
        y_c = jnp.dot(c_ref[rows, :], wc_ref[...], preferred_element_type=jnp.float32)
        a_in = (o_ref[rows, :].astype(jnp.float32)
                * sza_ref[rows, :].astype(jnp.float32)).astype(jnp.bfloat16)
        y_a = jnp.dot(a_in, wa_ref[...], preferred_element_type=jnp.float32)
        merged = (ga_ref[rows, :].astype(jnp.float32) * y_a
                  + gc_ref[rows, :].astype(jnp.float32) * y_c).astype(jnp.bfloat16)
        out_ref[rows, :] = x_ref[rows, :] + jnp.dot(merged, wo_ref[...],
                                                    preferred_element_type=jnp.float32)


def _tail(p_arr, o, x, wa, wc, wo, conv_w, conv_b, cn_g, cn_b, batch, seq):
    m = batch * seq
    ts = TAIL_TILE_M
    n_s = seq // ts

    def p_block(group):
        return pl.BlockSpec((ts, D_MODEL), lambda b, s: (b * n_s + s, group))

    row_block = pl.BlockSpec((ts, D_MODEL), lambda b, s: (b * n_s + s, 0))
    full = lambda shape: pl.BlockSpec(shape, lambda b, s: (0, 0))
    return pl.pallas_call(
        _tail_kernel,
        grid=(batch, n_s),
        in_specs=[
            p_block(P_GLU), p_block(P_SZC), row_block, p_block(P_SZA), p_block(P_GA),
            p_block(P_GC), row_block,
            full((D_MODEL, D_MODEL)), full((D_MODEL, D_MODEL)), full((D_MODEL, D_MODEL)),
            full((CONV_KERNEL, D_MODEL)), full((1, D_MODEL)), full((1, D_MODEL)),
            full((1, D_MODEL)),
        ],
        out_specs=row_block,
        out_shape=jax.ShapeDtypeStruct((m, D_MODEL), jnp.float32),
        scratch_shapes=[
            pltpu.VMEM((D_MODEL // LANES, CONV_HALO + ts, LANES), jnp.float32),
            pltpu.VMEM((ts, D_MODEL), jnp.bfloat16),
        ],
        compiler_params=pltpu.CompilerParams(
            dimension_semantics=("arbitrary", "arbitrary"),
            vmem_limit_bytes=VMEM_LIMIT),
        name="tail",
    )(p_arr, p_arr, o, p_arr, p_arr, p_arr, x, wa, wc, wo, conv_w, conv_b, cn_g, cn_b)


def _rope_tables(seq):
    inv_freq = ROPE_THETA ** (-jnp.arange(0, 2 * ROT_HALF, 2, dtype=jnp.float32) / (2 * ROT_HALF))
    plain = jnp.zeros((_HALF - ROT_HALF,), jnp.float32)
    lane_freq = jnp.concatenate([inv_freq, plain, inv_freq, plain])
    lane_sign = jnp.concatenate([-jnp.ones((_HALF,), jnp.float32), jnp.ones((_HALF,), jnp.float32)])
    ang = jnp.arange(seq, dtype=jnp.float32)[:, None] * lane_freq[None, :]
    return jnp.cos(ang), jnp.sin(ang) * lane_sign[None, :]


def _prepare_w_in_kernel(w_ref, qk_ref, rest_ref):
    g = pl.program_id(1)

    @pl.when(g <= W_K)
    def _():
        for h in range(N_HEADS):
            head = w_ref[:, h * HEAD_DIM:(h + 1) * HEAD_DIM]
            qk_ref[:, h * HEAD_DIM:(h + 1) * HEAD_DIM] = _permute_head_dims(head).astype(jnp.bfloat16)

    @pl.when(g > W_K)
    def _():
        rest_ref[...] = w_ref[...].astype(jnp.bfloat16)


def _prepare_w_in(w):
    depth = w.shape[0]
    n_rest = N_W_GROUPS - W_V
    return pl.pallas_call(
        _prepare_w_in_kernel,
        grid=(depth, N_W_GROUPS),
        in_specs=[pl.BlockSpec((None, D_MODEL, D_MODEL), lambda l, g: (l, 0, g))],
        out_specs=[
            pl.BlockSpec((None, D_MODEL, D_MODEL), lambda l, g: (l, 0, jnp.minimum(g, W_K))),
            pl.BlockSpec((None, D_MODEL, D_MODEL), lambda l, g: (l, 0, jnp.maximum(g - W_V, 0))),
        ],
        out_shape=[
            jax.ShapeDtypeStruct((depth, D_MODEL, 2 * D_MODEL), jnp.bfloat16),
            jax.ShapeDtypeStruct((depth, D_MODEL, n_rest * D_MODEL), jnp.bfloat16),
        ],
        compiler_params=pltpu.CompilerParams(
            dimension_semantics=("arbitrary", "arbitrary"),
            vmem_limit_bytes=VMEM_LIMIT),
        name="prepare_w_in",
    )(w)


def kernel(x, norm_g, w_in, b_gate, q_norm_g, k_norm_g, conv_w, conv_b, cn_g, cn_b,
           w_attn_proj, w_conv_proj, w_out):
    batch, seq, d = x.shape
    depth = w_in.shape[0]
    assert d == D_MODEL and seq % IN_TILE_M == 0 and seq % TAIL_TILE_M == 0
    cos_t, sin_t = _rope_tables(seq)
    bf = jnp.bfloat16
    w_qk, w_rest = _prepare_w_in(w_in)
    xf = x.reshape(batch * seq, d)
    for l in range(depth):
        p_arr, kmean, vt = _in_proj(
            l, xf, norm_g[l][None], w_qk, w_rest, b_gate[l][None],
            _permute_head_dims(q_norm_g[l])[None], _permute_head_dims(k_norm_g[l])[None],
            cos_t, sin_t, seq)
        kmean = kmean.reshape(batch, seq // MOBA_BLOCK, D_MODEL)
        score_bound = (LOG2_E * HEAD_DIM ** 0.5 * jnp.max(jnp.abs(q_norm_g[l]))
                       * jnp.max(jnp.abs(k_norm_g[l]))).reshape(1)
        o = _attention(score_bound, p_arr, vt, kmean, batch, seq)
        xf = _tail(p_arr, o, xf, w_attn_proj[l].astype(bf), w_conv_proj[l].astype(bf),
                   w_out[l].astype(bf), conv_w[l], conv_b[l][None], cn_g[l][None],
                   cn_b[l][None], batch, seq)
    return xf.reshape(batch, seq, d)
```

```python
import jax
import jax.numpy as jnp
from jax import lax
from jax.experimental import pallas as pl
from jax.experimental.pallas import tpu as pltpu

D_MODEL = 1024
N_HEADS = 8
HEAD_DIM = 128
ROT_HALF = 16
ROPE_THETA = 500000.0
MOBA_BLOCK = 256
MOBA_TOPK = 3
CONV_KERNEL = 31
CONV_HALO = 32
EPS = 1e-6
MASKED = -1e30
LOG2_E = 1.4426950408889634
ACC_ROWS = HEAD_DIM + 16
FIXED_SHIFT_MAX = 60.0
ATTN_TILE_Q = 512
SCORES_AHEAD_OWN = 4
SCORES_AHEAD_PAST = 6

LANES = 128
IN_TILE_M = 512
TAIL_TILE_M = 512
CONV_ROWS = 32
TAIL_MM_ROWS = 256
VMEM_LIMIT = 56 * 1024 * 1024
ATTN_VMEM_LIMIT = 60 * 1024 * 1024

P_Q, P_K, P_SZA, P_GLU, P_SZC, P_GA, P_GC = range(7)
N_P_GROUPS = 7
W_Q, W_K, W_V, W_ZA, W_UA, W_UB, W_ZC, W_GLA, W_GLC = range(9)
N_W_GROUPS = 9

_NT = (((1,), (1,)), ((), ()))

_HALF = HEAD_DIM // 2
_HEAD_PERM_SEGMENTS = ((0, ROT_HALF), (2 * ROT_HALF, _HALF + ROT_HALF),
                       (ROT_HALF, 2 * ROT_HALF), (_HALF + ROT_HALF, HEAD_DIM))


def _permute_head_dims(t):
    return jnp.concatenate([t[..., a:b] for a, b in _HEAD_PERM_SEGMENTS], axis=-1)


def _head_norm_rope(t, g, cos, sin):
    u = t * g
    rotated = u * cos + pltpu.roll(u, _HALF, 1) * sin
    ones = jnp.ones((HEAD_DIM, HEAD_DIM), jnp.bfloat16)
    ss = jnp.dot((t * t).astype(jnp.bfloat16), ones, preferred_element_type=jnp.float32)
    return rotated * lax.rsqrt(ss * (1.0 / HEAD_DIM) + EPS)


def _in_proj_kernel(x_ref, ng_ref, wqk_ref, w_ref, bg_ref, qg_ref, kg_ref, cos_ref, sin_ref,
                    p_ref, kmean_ref, vt_ref):
    tm = x_ref.shape[0]
    head_lanes = [slice(h * HEAD_DIM, (h + 1) * HEAD_DIM) for h in range(N_HEADS)]
    group_cols = lambda g: slice(g * D_MODEL, (g + 1) * D_MODEL)

    x = x_ref[...]
    ms = jnp.mean(x * x, axis=-1, keepdims=True)
    h = (x * lax.rsqrt(ms + EPS) * ng_ref[...]).astype(jnp.bfloat16)

    def product(g):
        if g <= W_K:
            w = wqk_ref[:, group_cols(g)]
        else:
            w = w_ref[:, group_cols(g - W_V)]
        return jnp.dot(h, w, preferred_element_type=jnp.float32)

    prod = {}

    def finish_q():
        scale = LOG2_E / (HEAD_DIM ** 0.5)
        for hl in head_lanes:
            t = _head_norm_rope(prod[W_Q][:, hl], qg_ref[...], cos_ref[...], sin_ref[...])
            p_ref[:, hl.start + P_Q * D_MODEL:hl.stop + P_Q * D_MODEL] = (t * scale).astype(jnp.bfloat16)

    def finish_k():
        for hl in head_lanes:
            t = _head_norm_rope(prod[W_K][:, hl], kg_ref[...], cos_ref[...], sin_ref[...])
            p_ref[:, hl.start + P_K * D_MODEL:hl.stop + P_K * D_MODEL] = t.astype(jnp.bfloat16)
            for c in range(tm // MOBA_BLOCK):
                blk = t[c * MOBA_BLOCK:(c + 1) * MOBA_BLOCK, :]
                kmean_ref[0, c:c + 1, hl] = jnp.mean(blk, axis=0, keepdims=True)

    def finish_v():
        for c in range(tm // MOBA_BLOCK):
            vt_ref[c] = prod[W_V][c * MOBA_BLOCK:(c + 1) * MOBA_BLOCK, :].T.astype(jnp.bfloat16)

    def silu(src, dst):
        a = prod[src]
        p_ref[:, group_cols(dst)] = (a * jax.nn.sigmoid(a)).astype(jnp.bfloat16)

    def finish_glu():
        p_ref[:, group_cols(P_GLU)] = (prod[W_UA] * jax.nn.sigmoid(prod[W_UB])).astype(jnp.bfloat16)

    def gate(src, dst, half):
        p_ref[:, group_cols(dst)] = jax.nn.sigmoid(
            prod[src] + bg_ref[:, group_cols(half)]).astype(jnp.bfloat16)

    finish = {
        W_Q: finish_q, W_K: finish_k, W_V: finish_v,
        W_ZA: lambda: silu(W_ZA, P_SZA),
        W_UA: lambda: None,
        W_UB: finish_glu,
        W_ZC: lambda: silu(W_ZC, P_SZC),
        W_GLA: lambda: gate(W_GLA, P_GA, 0),
        W_GLC: lambda: gate(W_GLC, P_GC, 1),
    }
    prod[0] = product(0)
    for g in range(N_W_GROUPS):
        if g + 1 < N_W_GROUPS:
            prod[g + 1] = product(g + 1)
        finish[g]()


def _in_proj(layer, x, norm_g, w_qk, w_rest, b_gate, q_g, k_g, cos, sin, seq):
    m = x.shape[0]
    tm = IN_TILE_M
    assert tm % MOBA_BLOCK == 0
    n_seq_tiles = seq // tm
    blocks_per_tile = tm // MOBA_BLOCK
    resident = pl.Buffered(1)
    const = lambda shape: pl.BlockSpec(shape, lambda i: (0, 0))

    return pl.pallas_call(
        _in_proj_kernel,
        grid=(m // tm,),
        in_specs=[
            pl.BlockSpec((tm, D_MODEL), lambda i: (i, 0)),
            const((1, D_MODEL)),
            pl.BlockSpec((None,) + w_qk.shape[1:], lambda i: (layer, 0, 0), pipeline_mode=resident),
            pl.BlockSpec((None,) + w_rest.shape[1:], lambda i: (layer, 0, 0), pipeline_mode=resident),
            const((1, 2 * D_MODEL)),
            const((1, HEAD_DIM)),
            const((1, HEAD_DIM)),
            pl.BlockSpec((tm, HEAD_DIM), lambda i: (i % n_seq_tiles, 0)),
            pl.BlockSpec((tm, HEAD_DIM), lambda i: (i % n_seq_tiles, 0)),
        ],
        out_specs=[
            pl.BlockSpec((tm, N_P_GROUPS * D_MODEL), lambda i: (i, 0)),
            pl.BlockSpec((1, blocks_per_tile, D_MODEL), lambda i: (i, 0, 0)),
            pl.BlockSpec((blocks_per_tile, D_MODEL, MOBA_BLOCK), lambda i: (i, 0, 0)),
        ],
        out_shape=[
            jax.ShapeDtypeStruct((m, N_P_GROUPS * D_MODEL), jnp.bfloat16),
            jax.ShapeDtypeStruct((m // tm, blocks_per_tile, D_MODEL), jnp.float32),
            jax.ShapeDtypeStruct((m // MOBA_BLOCK, D_MODEL, MOBA_BLOCK), jnp.bfloat16),
        ],
        compiler_params=pltpu.CompilerParams(
            dimension_semantics=("arbitrary",),
            vmem_limit_bytes=VMEM_LIMIT),
        name="in_proj",
    )(x, norm_g, w_qk, w_rest, b_gate, q_g, k_g, cos, sin)


def _attn_kernel(bound_ref, q_ref, k_ref, vt_ref, km_ref, o_ref, bias_ref, m_ref, acc_ref):
    n_blocks = km_ref.shape[1]
    tq = MOBA_BLOCK
    lax.fori_loop(0, q_ref.shape[0] // tq,
                  lambda sub, carry: _attn_query_block(
                      sub, bound_ref, q_ref, k_ref, vt_ref, km_ref, o_ref, bias_ref, m_ref, acc_ref),
                  0)


def _attn_query_block(sub, bound_ref, q_ref, k_ref, vt_ref, km_ref, o_ref, bias_ref, m_ref, acc_ref):
    n_blocks = km_ref.shape[1]
    tq = MOBA_BLOCK
    qi = pl.program_id(1) * (q_ref.shape[0] // tq) + sub
    q_rows = pl.ds(pl.multiple_of(sub * tq, tq), tq)
    ones_rows = jnp.ones((ACC_ROWS - HEAD_DIM, MOBA_BLOCK), jnp.bfloat16)
    head_lanes = [slice(h * HEAD_DIM, (h + 1) * HEAD_DIM) for h in range(N_HEADS)]
    bound = bound_ref[0]
    fixed_shift = bound <= FIXED_SHIFT_MAX

    def weighted_values(h, j, p):
        v_aug = jnp.concatenate([vt_ref[j, head_lanes[h], :], ones_rows], axis=0)
        return jnp.dot(v_aug, p.astype(jnp.bfloat16), preferred_element_type=jnp.float32)

    def scores(h, start):
        return lax.dot_general(k_ref[pl.ds(start, MOBA_BLOCK), head_lanes[h]],
                               q_ref[q_rows, head_lanes[h]], _NT,
                               preferred_element_type=jnp.float32)

    def for_each_head(start, consume):
        ahead = [scores(h, start) for h in range(SCORES_AHEAD_OWN)]
        for h in range(N_HEADS):
            s_t = ahead.pop(0)
            if h + SCORES_AHEAD_OWN < N_HEADS:
                ahead.append(scores(h + SCORES_AHEAD_OWN, start))
            consume(h, s_t)

    def select_blocks():
        blk_iota = lax.broadcasted_iota(jnp.int32, (n_blocks, tq), 0)
        past = blk_iota < qi
        gates = [lax.dot_general(km_ref[0, :, head_lanes[h]],
                                 q_ref[q_rows, head_lanes[h]].astype(jnp.float32), _NT,
                                 preferred_element_type=jnp.float32) for h in range(N_HEADS)]
        for h in range(N_HEADS):
            g = jnp.where(past, gates[h], -jnp.inf)
            chosen = jnp.zeros((n_blocks, tq), jnp.float32)
            for _ in range(MOBA_TOPK):
                top = jnp.max(g, axis=0, keepdims=True)
                first = jnp.min(jnp.where(g == top, blk_iota, n_blocks), axis=0, keepdims=True)
                pick = blk_iota == first
                chosen = jnp.where(pick, 1.0, chosen)
                g = jnp.where(pick, -jnp.inf, g)
            bias_ref[h] = jnp.where(past, jnp.where(chosen > 0.0, 0.0, MASKED), MASKED)

    key_i = lax.broadcasted_iota(jnp.int32, (MOBA_BLOCK, tq), 0)
    qry_i = lax.broadcasted_iota(jnp.int32, (MOBA_BLOCK, tq), 1)
    own = pl.multiple_of(qi * MOBA_BLOCK, MOBA_BLOCK)

    @pl.when(fixed_shift)
    def _():
        def own_block(h, s_t):
            p = jnp.exp2(jnp.where(key_i <= qry_i, s_t, MASKED) - bound)
            acc_ref[h] = weighted_values(h, qi, p)

        for_each_head(own, own_block)
        select_blocks()

        def probabilities(h, j, s_t):
            shift = bound - bias_ref[h, pl.ds(j, 1), :]
            return jnp.exp2(s_t - shift).astype(jnp.bfloat16)

        def group_step(j0, n):
            start = pl.multiple_of(j0 * MOBA_BLOCK, MOBA_BLOCK)
            ones = jnp.ones((ACC_ROWS - HEAD_DIM, n * MOBA_BLOCK), jnp.bfloat16)

            pairs = [(h, b) for h in range(N_HEADS) for b in range(n)]
            look = SCORES_AHEAD_PAST

            def pair_scores(idx):
                h, b = pairs[idx]
                return scores(h, start + b * MOBA_BLOCK)

            ahead = [pair_scores(i) for i in range(min(look, len(pairs)))]
            ps = []
            for idx, (h, b) in enumerate(pairs):
                s_t = ahead.pop(0)
                if idx + look < len(pairs):
                    ahead.append(pair_scores(idx + look))
                ps.append(probabilities(h, j0 + b, s_t))
                if b == n - 1:
                    p = jnp.concatenate(ps, axis=0)
                    ps = []
                    v_blocks = jnp.concatenate([vt_ref[j0 + bb, head_lanes[h], :] for bb in range(n)],
                                               axis=1)
                    v_aug = jnp.concatenate([v_blocks, ones], axis=0)
                    acc_ref[h] = acc_ref[h] + jnp.dot(v_aug, p, preferred_element_type=jnp.float32)

        def quad_body(i, carry):
            group_step(4 * i, 4)
            return carry

        lax.fori_loop(0, qi // 4, quad_body, 0)
        rest = (qi // 4) * 4

        @pl.when(qi % 4 >= 2)
        def _():
            group_step(rest, 2)

        @pl.when(qi % 2 == 1)
        def _():
            group_step(qi - 1, 1)

    @pl.when(jnp.logical_not(fixed_shift))
    def _():
        def own_block(h, s_t):
            s_t = jnp.where(key_i <= qry_i, s_t, MASKED)
            m0 = jnp.max(s_t, axis=0, keepdims=True)
            m_ref[h] = m0
            acc_ref[h] = weighted_values(h, qi, jnp.exp2(s_t - m0))

        for_each_head(own, own_block)
        select_blocks()

        def body(j, carry):
            def past_block(h, s_t):
                bias = bias_ref[h, pl.ds(j, 1), :]
                m_i = m_ref[h]
                m_new = jnp.maximum(m_i, jnp.max(s_t, axis=0, keepdims=True) + bias)
                alpha = jnp.exp2(m_i - m_new)
                p = jnp.exp2(s_t - (m_new - bias))
                m_ref[h] = m_new
                acc_ref[h] = alpha * acc_ref[h] + weighted_values(h, j, p)

            for_each_head(pl.multiple_of(j * MOBA_BLOCK, MOBA_BLOCK), past_block)
            return carry

        lax.fori_loop(0, qi, body, 0)

    for h in range(N_HEADS):
        acc = acc_ref[h]
        o_t = acc[:HEAD_DIM, :] / acc[HEAD_DIM:HEAD_DIM + 1, :]
        o_ref[q_rows, head_lanes[h]] = o_t.T.astype(o_ref.dtype)
    return 0


def _attention(score_bound, p_arr, vt, kmean, batch, seq):
    m = batch * seq
    n_blocks = seq // MOBA_BLOCK
    tq = MOBA_BLOCK
    rows = ATTN_TILE_Q
    n_steps = seq // rows
    width = N_HEADS * HEAD_DIM
    resident = pl.Buffered(1)
    return pl.pallas_call(
        _attn_kernel,
        grid=(batch, n_steps),
        in_specs=[
            pl.BlockSpec(memory_space=pltpu.SMEM),
            pl.BlockSpec((rows, width), lambda b, i: (b * n_steps + i, P_Q)),
            pl.BlockSpec((seq, width), lambda b, i: (b, P_K)),
            pl.BlockSpec((n_blocks, width, MOBA_BLOCK), lambda b, i: (b, 0, 0),
                         pipeline_mode=resident),
            pl.BlockSpec((1, n_blocks, width), lambda b, i: (b, 0, 0)),
        ],
        out_specs=pl.BlockSpec((rows, width), lambda b, i: (b * n_steps + i, 0)),
        out_shape=jax.ShapeDtypeStruct((m, width), jnp.bfloat16),
        scratch_shapes=[
            pltpu.VMEM((N_HEADS, n_blocks, tq), jnp.float32),
            pltpu.VMEM((N_HEADS, 1, tq), jnp.float32),
            pltpu.VMEM((N_HEADS, ACC_ROWS, tq), jnp.float32),
        ],
        compiler_params=pltpu.CompilerParams(
            dimension_semantics=("arbitrary", "arbitrary"),
            vmem_limit_bytes=ATTN_VMEM_LIMIT),
        name="moba_attn",
    )(score_bound, p_arr, p_arr, vt, kmean)


def _tail_kernel(glu_ref, szc_ref, o_ref, sza_ref, ga_ref, gc_ref, x_ref,
                 wa_ref, wc_ref, wo_ref, cw_ref, cb_ref, cng_ref, cnb_ref,
                 out_ref, hist_ref, c_ref):
    s = pl.program_id(1)
    ts = x_ref.shape[0]
    n_lane_tiles = D_MODEL // LANES
    lane_tiles = [slice(c * LANES, (c + 1) * LANES) for c in range(n_lane_tiles)]

    @pl.when(s == 0)
    def _():
        for c in range(n_lane_tiles):
            hist_ref[c, 0:CONV_HALO, :] = jnp.zeros((CONV_HALO, LANES), jnp.float32)

    @pl.when(s > 0)
    def _():
        for c in range(n_lane_tiles):
            hist_ref[c, 0:CONV_HALO, :] = hist_ref[c, ts:ts + CONV_HALO, :]

    for c in range(n_lane_tiles):
        hist_ref[c, CONV_HALO:, :] = glu_ref[:, lane_tiles[c]].astype(jnp.float32)

    first_tap = CONV_HALO - (CONV_KERNEL - 1)

    def conv_norm_chunk(r0):
        parts = []
        for c in range(n_lane_tiles):
            acc = hist_ref[c, r0 + first_tap:r0 + first_tap + CONV_ROWS, :] * cw_ref[0:1, lane_tiles[c]]
            for t in range(1, CONV_KERNEL):
                acc = acc + (hist_ref[c, r0 + first_tap + t:r0 + first_tap + t + CONV_ROWS, :]
                             * cw_ref[t:t + 1, lane_tiles[c]])
            parts.append(acc)
        conv = jnp.concatenate(parts, axis=1) + cb_ref[...]
        mu = jnp.mean(conv, axis=-1, keepdims=True)
        d = conv - mu
        var = jnp.mean(d * d, axis=-1, keepdims=True)
        y = d * lax.rsqrt(var + EPS) * cng_ref[...] + cnb_ref[...]
        act = y * jax.nn.sigmoid(y)
        gate = szc_ref[r0:r0 + CONV_ROWS, :].astype(jnp.float32)
        c_ref[r0:r0 + CONV_ROWS, :] = (act * gate).astype(jnp.bfloat16)

    for m0 in range(0, ts, TAIL_MM_ROWS):
        for r0 in range(m0, m0 + TAIL_MM_ROWS, CONV_ROWS):
            conv_norm_chunk(r0)
        rows = slice(m0, m0 + TAIL_MM_ROWS)
        y_c = jnp.dot(c_ref[rows, :], wc_ref[...], preferred_element_type=jnp.float32)
        a_in = (o_ref[rows, :].astype(jnp.float32)
                * sza_ref[rows, :].astype(jnp.float32)).astype(jnp.bfloat16)
        y_a = jnp.dot(a_in, wa_ref[...], preferred_element_type=jnp.float32)
        merged = (ga_ref[rows, :].astype(jnp.float32) * y_a
                  + gc_ref[rows, :].astype(jnp.float32) * y_c).astype(jnp.bfloat16)
        out_ref[rows, :] = x_ref[rows, :] + jnp.dot(merged, wo_ref[...],
                                                    preferred_element_type=jnp.float32)


def _tail(p_arr, o, x, wa, wc, wo, conv_w, conv_b, cn_g, cn_b, batch, seq):
    m = batch * seq
    ts = TAIL_TILE_M
    n_s = seq // ts

    def p_block(group):
        return pl.BlockSpec((ts, D_MODEL), lambda b, s: (b * n_s + s, group))

    row_block = pl.BlockSpec((ts, D_MODEL), lambda b, s: (b * n_s + s, 0))
    full = lambda shape: pl.BlockSpec(shape, lambda b, s: (0, 0))
    return pl.pallas_call(
        _tail_kernel,
        grid=(batch, n_s),
        in_specs=[
            p_block(P_GLU), p_block(P_SZC), row_block, p_block(P_SZA), p_block(P_GA),
            p_block(P_GC), row_block,
            full((D_MODEL, D_MODEL)), full((D_MODEL, D_MODEL)), full((D_MODEL, D_MODEL)),
            full((CONV_KERNEL, D_MODEL)), full((1, D_MODEL)), full((1, D_MODEL)),
            full((1, D_MODEL)),
        ],
        out_specs=row_block,
        out_shape=jax.ShapeDtypeStruct((m, D_MODEL), jnp.float32),
        scratch_shapes=[
            pltpu.VMEM((D_MODEL // LANES, CONV_HALO + ts, LANES), jnp.float32),
            pltpu.VMEM((ts, D_MODEL), jnp.bfloat16),
        ],
        compiler_params=pltpu.CompilerParams(
            dimension_semantics=("arbitrary", "arbitrary"),
            vmem_limit_bytes=VMEM_LIMIT),
        name="tail",
    )(p_arr, p_arr, o, p_arr, p_arr, p_arr, x, wa, wc, wo, conv_w, conv_b, cn_g, cn_b)


def _rope_tables(seq):
    inv_freq = ROPE_THETA ** (-jnp.arange(0, 2 * ROT_HALF, 2, dtype=jnp.float32) / (2 * ROT_HALF))
    ang = jnp.arange(seq, dtype=jnp.float32)[:, None] * inv_freq[None, :]
    no_freq = jnp.full((_HALF - ROT_HALF,), -1, jnp.int32)
    freq_of_lane = jnp.concatenate([jnp.arange(ROT_HALF), no_freq, jnp.arange(ROT_HALF), no_freq])
    spread = (freq_of_lane[None, :] == jnp.arange(ROT_HALF)[:, None]).astype(jnp.float32)
    plain = (freq_of_lane < 0).astype(jnp.float32)
    lane_sign = jnp.concatenate([-jnp.ones((_HALF,), jnp.float32), jnp.ones((_HALF,), jnp.float32)])
    exact = lax.Precision.HIGHEST
    cos_t = jnp.dot(jnp.cos(ang), spread, precision=exact) + plain[None, :]
    sin_t = jnp.dot(jnp.sin(ang), spread * lane_sign[None, :], precision=exact)
    return cos_t, sin_t


def _prepare_w_in_kernel(w_ref, qk_ref, rest_ref):
    g = pl.program_id(1)

    @pl.when(g <= W_K)
    def _():
        for h in range(N_HEADS):
            head = w_ref[:, h * HEAD_DIM:(h + 1) * HEAD_DIM]
            qk_ref[:, h * HEAD_DIM:(h + 1) * HEAD_DIM] = _permute_head_dims(head).astype(jnp.bfloat16)

    @pl.when(g > W_K)
    def _():
        rest_ref[...] = w_ref[...].astype(jnp.bfloat16)


def _prepare_w_in(w):
    depth = w.shape[0]
    n_rest = N_W_GROUPS - W_V
    return pl.pallas_call(
        _prepare_w_in_kernel,
        grid=(depth, N_W_GROUPS),
        in_specs=[pl.BlockSpec((None, D_MODEL, D_MODEL), lambda l, g: (l, 0, g))],
        out_specs=[
            pl.BlockSpec((None, D_MODEL, D_MODEL), lambda l, g: (l, 0, jnp.minimum(g, W_K))),
            pl.BlockSpec((None, D_MODEL, D_MODEL), lambda l, g: (l, 0, jnp.maximum(g - W_V, 0))),
        ],
        out_shape=[
            jax.ShapeDtypeStruct((depth, D_MODEL, 2 * D_MODEL), jnp.bfloat16),
            jax.ShapeDtypeStruct((depth, D_MODEL, n_rest * D_MODEL), jnp.bfloat16),
        ],
        compiler_params=pltpu.CompilerParams(
            dimension_semantics=("arbitrary", "arbitrary"),
            vmem_limit_bytes=VMEM_LIMIT),
        name="prepare_w_in",
    )(w)


def kernel(x, norm_g, w_in, b_gate, q_norm_g, k_norm_g, conv_w, conv_b, cn_g, cn_b,
           w_attn_proj, w_conv_proj, w_out):
    batch, seq, d = x.shape
    depth = w_in.shape[0]
    assert d == D_MODEL and seq % IN_TILE_M == 0 and seq % TAIL_TILE_M == 0
    cos_t, sin_t = _rope_tables(seq)
    bf = jnp.bfloat16
    w_qk, w_rest = _prepare_w_in(w_in)
    xf = x.reshape(batch * seq, d)
    for l in range(depth):
        p_arr, kmean, vt = _in_proj(
            l, xf, norm_g[l][None], w_qk, w_rest, b_gate[l][None],
            _permute_head_dims(q_norm_g[l])[None], _permute_head_dims(k_norm_g[l])[None],
            cos_t, sin_t, seq)
        kmean = kmean.reshape(batch, seq // MOBA_BLOCK, D_MODEL)
        score_bound = (LOG2_E * HEAD_DIM ** 0.5 * jnp.max(jnp.abs(q_norm_g[l]))
                       * jnp.max(jnp.abs(k_norm_g[l]))).reshape(1)
        o = _attention(score_bound, p_arr, vt, kmean, batch, seq)
        xf = _tail(p_arr, o, xf, w_attn_proj[l].astype(bf), w_conv_proj[l].astype(bf),
                   w_out[l].astype(bf), conv_w[l], conv_b[l][None], cn_g[l][None],
                   cn_b[l][None], batch, seq)
    return xf.reshape(batch, seq, d)
```

```python
import jax
import jax.numpy as jnp
from jax import lax
from jax.experimental import pallas as pl
from jax.experimental.pallas import tpu as pltpu

D_MODEL = 1024
N_HEADS = 8
HEAD_DIM = 128
ROT_HALF = 16
ROPE_THETA = 500000.0
MOBA_BLOCK = 256
MOBA_TOPK = 3
CONV_KERNEL = 31
CONV_HALO = 32
EPS = 1e-6
MASKED = -1e30
LOG2_E = 1.4426950408889634
ACC_ROWS = HEAD_DIM + 16
FIXED_SHIFT_MAX = 60.0
ATTN_TILE_Q = 512
SCORES_AHEAD_OWN = 4
SCORES_AHEAD_PAST = 6

LANES = 128
IN_TILE_M = 512
TAIL_TILE_M = 512
CONV_ROWS = 64
NORM_ROWS = 16
TAIL_MM_ROWS = 256
VMEM_LIMIT = 56 * 1024 * 1024
ATTN_VMEM_LIMIT = 60 * 1024 * 1024

P_Q, P_K, P_SZA, P_GLU, P_SZC, P_GA, P_GC = range(7)
N_P_GROUPS = 7
W_Q, W_K, W_V, W_ZA, W_UA, W_UB, W_ZC, W_GLA, W_GLC = range(9)
N_W_GROUPS = 9

_NT = (((1,), (1,)), ((), ()))

_HALF = HEAD_DIM // 2
_HEAD_PERM_SEGMENTS = ((0, ROT_HALF), (2 * ROT_HALF, _HALF + ROT_HALF),
                       (ROT_HALF, 2 * ROT_HALF), (_HALF + ROT_HALF, HEAD_DIM))


def _permute_head_dims(t):
    return jnp.concatenate([t[..., a:b] for a, b in _HEAD_PERM_SEGMENTS], axis=-1)


def _head_norm_rope(t, g, cos, sin):
    u = t * g
    rotated = u * cos + pltpu.roll(u, _HALF, 1) * sin
    ones = jnp.ones((HEAD_DIM, HEAD_DIM), jnp.bfloat16)
    ss = jnp.dot((t * t).astype(jnp.bfloat16), ones, preferred_element_type=jnp.float32)
    return rotated * lax.rsqrt(ss * (1.0 / HEAD_DIM) + EPS)


def _in_proj_kernel(x_ref, ng_ref, wqk_ref, w_ref, bg_ref, qg_ref, kg_ref, cos_ref, sin_ref,
                    p_ref, kmean_ref, vt_ref):
    tm = x_ref.shape[0]
    head_lanes = [slice(h * HEAD_DIM, (h + 1) * HEAD_DIM) for h in range(N_HEADS)]
    group_cols = lambda g: slice(g * D_MODEL, (g + 1) * D_MODEL)

    x = x_ref[...]
    ms = jnp.mean(x * x, axis=-1, keepdims=True)
    h = (x * lax.rsqrt(ms + EPS) * ng_ref[...]).astype(jnp.bfloat16)

    def product(g):
        if g <= W_K:
            w = wqk_ref[:, group_cols(g)]
        else:
            w = w_ref[:, group_cols(g - W_V)]
        return jnp.dot(h, w, preferred_element_type=jnp.float32)

    prod = {}

    def finish_q():
        scale = LOG2_E / (HEAD_DIM ** 0.5)
        for hl in head_lanes:
            t = _head_norm_rope(prod[W_Q][:, hl], qg_ref[...], cos_ref[...], sin_ref[...])
            p_ref[:, hl.start + P_Q * D_MODEL:hl.stop + P_Q * D_MODEL] = (t * scale).astype(jnp.bfloat16)

    def finish_k():
        for hl in head_lanes:
            t = _head_norm_rope(prod[W_K][:, hl], kg_ref[...], cos_ref[...], sin_ref[...])
            p_ref[:, hl.start + P_K * D_MODEL:hl.stop + P_K * D_MODEL] = t.astype(jnp.bfloat16)
            for c in range(tm // MOBA_BLOCK):
                blk = t[c * MOBA_BLOCK:(c + 1) * MOBA_BLOCK, :]
                kmean_ref[0, c:c + 1, hl] = jnp.mean(blk, axis=0, keepdims=True)

    def finish_v():
        for c in range(tm // MOBA_BLOCK):
            vt_ref[c] = prod[W_V][c * MOBA_BLOCK:(c + 1) * MOBA_BLOCK, :].T.astype(jnp.bfloat16)

    def silu(src, dst):
        a = prod[src]
        p_ref[:, group_cols(dst)] = (a * jax.nn.sigmoid(a)).astype(jnp.bfloat16)

    def finish_glu():
        p_ref[:, group_cols(P_GLU)] = (prod[W_UA] * jax.nn.sigmoid(prod[W_UB])).astype(jnp.bfloat16)

    def gate(src, dst, half):
        p_ref[:, group_cols(dst)] = jax.nn.sigmoid(
            prod[src] + bg_ref[:, group_cols(half)]).astype(jnp.bfloat16)

    finish = {
        W_Q: finish_q, W_K: finish_k, W_V: finish_v,
        W_ZA: lambda: silu(W_ZA, P_SZA),
        W_UA: lambda: None,
        W_UB: finish_glu,
        W_ZC: lambda: silu(W_ZC, P_SZC),
        W_GLA: lambda: gate(W_GLA, P_GA, 0),
        W_GLC: lambda: gate(W_GLC, P_GC, 1),
    }
    prod[0] = product(0)
    for g in range(N_W_GROUPS):
        if g + 1 < N_W_GROUPS:
            prod[g + 1] = product(g + 1)
        finish[g]()


def _in_proj(layer, x, norm_g, w_qk, w_rest, b_gate, q_g, k_g, cos, sin, seq):
    m = x.shape[0]
    tm = IN_TILE_M
    assert tm % MOBA_BLOCK == 0
    n_seq_tiles = seq // tm
    blocks_per_tile = tm // MOBA_BLOCK
    resident = pl.Buffered(1)
    const = lambda shape: pl.BlockSpec(shape, lambda i: (0, 0))

    return pl.pallas_call(
        _in_proj_kernel,
        grid=(m // tm,),
        in_specs=[
            pl.BlockSpec((tm, D_MODEL), lambda i: (i, 0)),
            const((1, D_MODEL)),
            pl.BlockSpec((None,) + w_qk.shape[1:], lambda i: (layer, 0, 0), pipeline_mode=resident),
            pl.BlockSpec((None,) + w_rest.shape[1:], lambda i: (layer, 0, 0), pipeline_mode=resident),
            const((1, 2 * D_MODEL)),
            const((1, HEAD_DIM)),
            const((1, HEAD_DIM)),
            pl.BlockSpec((tm, HEAD_DIM), lambda i: (i % n_seq_tiles, 0)),
            pl.BlockSpec((tm, HEAD_DIM), lambda i: (i % n_seq_tiles, 0)),
        ],
        out_specs=[
            pl.BlockSpec((tm, N_P_GROUPS * D_MODEL), lambda i: (i, 0)),
            pl.BlockSpec((1, blocks_per_tile, D_MODEL), lambda i: (i, 0, 0)),
            pl.BlockSpec((blocks_per_tile, D_MODEL, MOBA_BLOCK), lambda i: (i, 0, 0)),
        ],
        out_shape=[
            jax.ShapeDtypeStruct((m, N_P_GROUPS * D_MODEL), jnp.bfloat16),
            jax.ShapeDtypeStruct((m // tm, blocks_per_tile, D_MODEL), jnp.float32),
            jax.ShapeDtypeStruct((m // MOBA_BLOCK, D_MODEL, MOBA_BLOCK), jnp.bfloat16),
        ],
        compiler_params=pltpu.CompilerParams(
            dimension_semantics=("arbitrary",),
            vmem_limit_bytes=VMEM_LIMIT),
        name="in_proj",
    )(x, norm_g, w_qk, w_rest, b_gate, q_g, k_g, cos, sin)


def _attn_kernel(bound_ref, q_ref, k_ref, vt_ref, km_ref, o_ref, bias_ref, m_ref, acc_ref):
    n_blocks = km_ref.shape[1]
    tq = MOBA_BLOCK
    lax.fori_loop(0, q_ref.shape[0] // tq,
                  lambda sub, carry: _attn_query_block(
                      sub, bound_ref, q_ref, k_ref, vt_ref, km_ref, o_ref, bias_ref, m_ref, acc_ref),
                  0)


def _attn_query_block(sub, bound_ref, q_ref, k_ref, vt_ref, km_ref, o_ref, bias_ref, m_ref, acc_ref):
    n_blocks = km_ref.shape[1]
    tq = MOBA_BLOCK
    qi = pl.program_id(1) * (q_ref.shape[0] // tq) + sub
    q_rows = pl.ds(pl.multiple_of(sub * tq, tq), tq)
    ones_rows = jnp.ones((ACC_ROWS - HEAD_DIM, MOBA_BLOCK), jnp.bfloat16)
    head_lanes = [slice(h * HEAD_DIM, (h + 1) * HEAD_DIM) for h in range(N_HEADS)]
    bound = bound_ref[0]
    fixed_shift = bound <= FIXED_SHIFT_MAX

    def weighted_values(h, j, p):
        v_aug = jnp.concatenate([vt_ref[j, head_lanes[h], :], ones_rows], axis=0)
        return jnp.dot(v_aug, p.astype(jnp.bfloat16), preferred_element_type=jnp.float32)

    def scores(h, start):
        return lax.dot_general(k_ref[pl.ds(start, MOBA_BLOCK), head_lanes[h]],
                               q_ref[q_rows, head_lanes[h]], _NT,
                               preferred_element_type=jnp.float32)

    def for_each_head(start, consume):
        ahead = [scores(h, start) for h in range(SCORES_AHEAD_OWN)]
        for h in range(N_HEADS):
            s_t = ahead.pop(0)
            if h + SCORES_AHEAD_OWN < N_HEADS:
                ahead.append(scores(h + SCORES_AHEAD_OWN, start))
            consume(h, s_t)

    def select_blocks():
        blk_iota = lax.broadcasted_iota(jnp.int32, (n_blocks, tq), 0)
        past = blk_iota < qi
        gates = [lax.dot_general(km_ref[0, :, head_lanes[h]],
                                 q_ref[q_rows, head_lanes[h]].astype(jnp.float32), _NT,
                                 preferred_element_type=jnp.float32) for h in range(N_HEADS)]
        for h in range(N_HEADS):
            g = jnp.where(past, gates[h], -jnp.inf)
            chosen = jnp.zeros((n_blocks, tq), jnp.float32)
            for _ in range(MOBA_TOPK):
                top = jnp.max(g, axis=0, keepdims=True)
                first = jnp.min(jnp.where(g == top, blk_iota, n_blocks), axis=0, keepdims=True)
                pick = blk_iota == first
                chosen = jnp.where(pick, 1.0, chosen)
                g = jnp.where(pick, -jnp.inf, g)
            bias_ref[h] = jnp.where(past, jnp.where(chosen > 0.0, 0.0, MASKED), MASKED)

    key_i = lax.broadcasted_iota(jnp.int32, (MOBA_BLOCK, tq), 0)
    qry_i = lax.broadcasted_iota(jnp.int32, (MOBA_BLOCK, tq), 1)
    own = pl.multiple_of(qi * MOBA_BLOCK, MOBA_BLOCK)

    @pl.when(fixed_shift)
    def _():
        def own_block(h, s_t):
            p = jnp.exp2(jnp.where(key_i <= qry_i, s_t, MASKED) - bound)
            acc_ref[h] = weighted_values(h, qi, p)

        for_each_head(own, own_block)
        select_blocks()

        def probabilities(h, j, s_t):
            shift = bound - bias_ref[h, pl.ds(j, 1), :]
            return jnp.exp2(s_t - shift).astype(jnp.bfloat16)

        def group_step(j0, n):
            start = pl.multiple_of(j0 * MOBA_BLOCK, MOBA_BLOCK)
            ones = jnp.ones((ACC_ROWS - HEAD_DIM, n * MOBA_BLOCK), jnp.bfloat16)

            pairs = [(h, b) for h in range(N_HEADS) for b in range(n)]
            look = SCORES_AHEAD_PAST

            def pair_scores(idx):
                h, b = pairs[idx]
                return scores(h, start + b * MOBA_BLOCK)

            ahead = [pair_scores(i) for i in range(min(look, len(pairs)))]
            ps = []
            for idx, (h, b) in enumerate(pairs):
                s_t = ahead.pop(0)
                if idx + look < len(pairs):
                    ahead.append(pair_scores(idx + look))
                ps.append(probabilities(h, j0 + b, s_t))
                if b == n - 1:
                    p = jnp.concatenate(ps, axis=0)
                    ps = []
                    v_blocks = jnp.concatenate([vt_ref[j0 + bb, head_lanes[h], :] for bb in range(n)],
                                               axis=1)
                    v_aug = jnp.concatenate([v_blocks, ones], axis=0)
                    acc_ref[h] = acc_ref[h] + jnp.dot(v_aug, p, preferred_element_type=jnp.float32)

        def quad_body(i, carry):
            group_step(4 * i, 4)
            return carry

        lax.fori_loop(0, qi // 4, quad_body, 0)
        rest = (qi // 4) * 4

        @pl.when(qi % 4 >= 2)
        def _():
            group_step(rest, 2)

        @pl.when(qi % 2 == 1)
        def _():
            group_step(qi - 1, 1)

    @pl.when(jnp.logical_not(fixed_shift))
    def _():
        def own_block(h, s_t):
            s_t = jnp.where(key_i <= qry_i, s_t, MASKED)
            m0 = jnp.max(s_t, axis=0, keepdims=True)
            m_ref[h] = m0
            acc_ref[h] = weighted_values(h, qi, jnp.exp2(s_t - m0))

        for_each_head(own, own_block)
        select_blocks()

        def body(j, carry):
            def past_block(h, s_t):
                bias = bias_ref[h, pl.ds(j, 1), :]
                m_i = m_ref[h]
                m_new = jnp.maximum(m_i, jnp.max(s_t, axis=0, keepdims=True) + bias)
                alpha = jnp.exp2(m_i - m_new)
                p = jnp.exp2(s_t - (m_new - bias))
                m_ref[h] = m_new
                acc_ref[h] = alpha * acc_ref[h] + weighted_values(h, j, p)

            for_each_head(pl.multiple_of(j * MOBA_BLOCK, MOBA_BLOCK), past_block)
            return carry

        lax.fori_loop(0, qi, body, 0)

    for h in range(N_HEADS):
        acc = acc_ref[h]
        o_t = acc[:HEAD_DIM, :] / acc[HEAD_DIM:HEAD_DIM + 1, :]
        o_ref[q_rows, head_lanes[h]] = o_t.T.astype(o_ref.dtype)
    return 0


def _attention(score_bound, p_arr, vt, kmean, batch, seq):
    m = batch * seq
    n_blocks = seq // MOBA_BLOCK
    tq = MOBA_BLOCK
    rows = ATTN_TILE_Q
    n_steps = seq // rows
    width = N_HEADS * HEAD_DIM
    resident = pl.Buffered(1)
    return pl.pallas_call(
        _attn_kernel,
        grid=(batch, n_steps),
        in_specs=[
            pl.BlockSpec(memory_space=pltpu.SMEM),
            pl.BlockSpec((rows, width), lambda b, i: (b * n_steps + i, P_Q)),
            pl.BlockSpec((seq, width), lambda b, i: (b, P_K)),
            pl.BlockSpec((n_blocks, width, MOBA_BLOCK), lambda b, i: (b, 0, 0),
                         pipeline_mode=resident),
            pl.BlockSpec((1, n_blocks, width), lambda b, i: (b, 0, 0)),
        ],
        out_specs=pl.BlockSpec((rows, width), lambda b, i: (b * n_steps + i, 0)),
        out_shape=jax.ShapeDtypeStruct((m, width), jnp.bfloat16),
        scratch_shapes=[
            pltpu.VMEM((N_HEADS, n_blocks, tq), jnp.float32),
            pltpu.VMEM((N_HEADS, 1, tq), jnp.float32),
            pltpu.VMEM((N_HEADS, ACC_ROWS, tq), jnp.float32),
        ],
        compiler_params=pltpu.CompilerParams(
            dimension_semantics=("arbitrary", "arbitrary"),
            vmem_limit_bytes=ATTN_VMEM_LIMIT),
        name="moba_attn",
    )(score_bound, p_arr, p_arr, vt, kmean)


def _tail_kernel(glu_ref, szc_ref, o_ref, sza_ref, ga_ref, gc_ref, x_ref,
                 wa_ref, wc_ref, wo_ref, cw_ref, cb_ref, cng_ref, cnb_ref,
                 out_ref, hist_ref, conv_ref, c_ref):
    s = pl.program_id(1)
    ts = x_ref.shape[0]
    n_lane_tiles = D_MODEL // LANES
    lane_tiles = [slice(c * LANES, (c + 1) * LANES) for c in range(n_lane_tiles)]

    @pl.when(s == 0)
    def _():
        for c in range(n_lane_tiles):
            hist_ref[c, 0:CONV_HALO, :] = jnp.zeros((CONV_HALO, LANES), jnp.float32)

    @pl.when(s > 0)
    def _():
        for c in range(n_lane_tiles):
            hist_ref[c, 0:CONV_HALO, :] = hist_ref[c, ts:ts + CONV_HALO, :]

    for c in range(n_lane_tiles):
        hist_ref[c, CONV_HALO:, :] = glu_ref[:, lane_tiles[c]].astype(jnp.float32)

    first_tap = CONV_HALO - (CONV_KERNEL - 1)

    def conv_norm_chunk(r0):
        for c in range(n_lane_tiles):
            acc = hist_ref[c, r0 + first_tap:r0 + first_tap + CONV_ROWS, :] * cw_ref[0:1, lane_tiles[c]]
            for t in range(1, CONV_KERNEL):
                acc = acc + (hist_ref[c, r0 + first_tap + t:r0 + first_tap + t + CONV_ROWS, :]
                             * cw_ref[t:t + 1, lane_tiles[c]])
            conv_ref[r0:r0 + CONV_ROWS, lane_tiles[c]] = acc + cb_ref[:, lane_tiles[c]]
        for n0 in range(r0, r0 + CONV_ROWS, NORM_ROWS):
            conv = conv_ref[n0:n0 + NORM_ROWS, :]
            mu = jnp.mean(conv, axis=-1, keepdims=True)
            d = conv - mu
            var = jnp.mean(d * d, axis=-1, keepdims=True)
            y = d * lax.rsqrt(var + EPS) * cng_ref[...] + cnb_ref[...]
            act = y * jax.nn.sigmoid(y)
            gate = szc_ref[n0:n0 + NORM_ROWS, :].astype(jnp.float32)
            c_ref[n0:n0 + NORM_ROWS, :] = (act * gate).astype(jnp.bfloat16)

    for m0 in range(0, ts, TAIL_MM_ROWS):
        for r0 in range(m0, m0 + TAIL_MM_ROWS, CONV_ROWS):
            conv_norm_chunk(r0)
        rows = slice(m0, m0 + TAIL_MM_ROWS)
        y_c = jnp.dot(c_ref[rows, :], wc_ref[...], preferred_element_type=jnp.float32)
        a_in = (o_ref[rows, :].astype(jnp.float32)
                * sza_ref[rows, :].astype(jnp.float32)).astype(jnp.bfloat16)
        y_a = jnp.dot(a_in, wa_ref[...], preferred_element_type=jnp.float32)
        merged = (ga_ref[rows, :].astype(jnp.float32) * y_a
                  + gc_ref[rows, :].astype(jnp.float32) * y_c).astype(jnp.bfloat16)
        out_ref[rows, :] = x_ref[rows, :] + jnp.dot(merged, wo_ref[...],
                                                    preferred_element_type=jnp.float32)


def _tail(p_arr, o, x, wa, wc, wo, conv_w, conv_b, cn_g, cn_b, batch, seq):
    m = batch * seq
    ts = TAIL_TILE_M
    n_s = seq // ts

    def p_block(group):
        return pl.BlockSpec((ts, D_MODEL), lambda b, s: (b * n_s + s, group))

    row_block = pl.BlockSpec((ts, D_MODEL), lambda b, s: (b * n_s + s, 0))
    full = lambda shape: pl.BlockSpec(shape, lambda b, s: (0, 0))
    return pl.pallas_call(
        _tail_kernel,
        grid=(batch, n_s),
        in_specs=[
            p_block(P_GLU), p_block(P_SZC), row_block, p_block(P_SZA), p_block(P_GA),
            p_block(P_GC), row_block,
            full((D_MODEL, D_MODEL)), full((D_MODEL, D_MODEL)), full((D_MODEL, D_MODEL)),
            full((CONV_KERNEL, D_MODEL)), full((1, D_MODEL)), full((1, D_MODEL)),
            full((1, D_MODEL)),
        ],
        out_specs=row_block,
        out_shape=jax.ShapeDtypeStruct((m, D_MODEL), jnp.float32),
        scratch_shapes=[
            pltpu.VMEM((D_MODEL // LANES, CONV_HALO + ts, LANES), jnp.float32),
            pltpu.VMEM((ts, D_MODEL), jnp.float32),
            pltpu.VMEM((ts, D_MODEL), jnp.bfloat16),
        ],
        compiler_params=pltpu.CompilerParams(
            dimension_semantics=("arbitrary", "arbitrary"),
            vmem_limit_bytes=VMEM_LIMIT),
        name="tail",
    )(p_arr, p_arr, o, p_arr, p_arr, p_arr, x, wa, wc, wo, conv_w, conv_b, cn_g, cn_b)


def _rope_tables(seq):
    inv_freq = ROPE_THETA ** (-jnp.arange(0, 2 * ROT_HALF, 2, dtype=jnp.float32) / (2 * ROT_HALF))
    ang = jnp.arange(seq, dtype=jnp.float32)[:, None] * inv_freq[None, :]
    no_freq = jnp.full((_HALF - ROT_HALF,), -1, jnp.int32)
    freq_of_lane = jnp.concatenate([jnp.arange(ROT_HALF), no_freq, jnp.arange(ROT_HALF), no_freq])
    spread = (freq_of_lane[None, :] == jnp.arange(ROT_HALF)[:, None]).astype(jnp.float32)
    plain = (freq_of_lane < 0).astype(jnp.float32)
    lane_sign = jnp.concatenate([-jnp.ones((_HALF,), jnp.float32), jnp.ones((_HALF,), jnp.float32)])
    exact = lax.Precision.HIGHEST
    cos_t = jnp.dot(jnp.cos(ang), spread, precision=exact) + plain[None, :]
    sin_t = jnp.dot(jnp.sin(ang), spread * lane_sign[None, :], precision=exact)
    return cos_t, sin_t


def _prepare_w_in_kernel(w_ref, qk_ref, rest_ref):
    g = pl.program_id(1)

    @pl.when(g <= W_K)
    def _():
        for h in range(N_HEADS):
            head = w_ref[:, h * HEAD_DIM:(h + 1) * HEAD_DIM]
            qk_ref[:, h * HEAD_DIM:(h + 1) * HEAD_DIM] = _permute_head_dims(head).astype(jnp.bfloat16)

    @pl.when(g > W_K)
    def _():
        rest_ref[...] = w_ref[...].astype(jnp.bfloat16)


def _prepare_w_in(w):
    depth = w.shape[0]
    n_rest = N_W_GROUPS - W_V
    return pl.pallas_call(
        _prepare_w_in_kernel,
        grid=(depth, N_W_GROUPS),
        in_specs=[pl.BlockSpec((None, D_MODEL, D_MODEL), lambda l, g: (l, 0, g))],
        out_specs=[
            pl.BlockSpec((None, D_MODEL, D_MODEL), lambda l, g: (l, 0, jnp.minimum(g, W_K))),
            pl.BlockSpec((None, D_MODEL, D_MODEL), lambda l, g: (l, 0, jnp.maximum(g - W_V, 0))),
        ],
        out_shape=[
            jax.ShapeDtypeStruct((depth, D_MODEL, 2 * D_MODEL), jnp.bfloat16),
            jax.ShapeDtypeStruct((depth, D_MODEL, n_rest * D_MODEL), jnp.bfloat16),
        ],
        compiler_params=pltpu.CompilerParams(
            dimension_semantics=("arbitrary", "arbitrary"),
            vmem_limit_bytes=VMEM_LIMIT),
        name="prepare_w_in",
    )(w)


def kernel(x, norm_g, w_in, b_gate, q_norm_g, k_norm_g, conv_w, conv_b, cn_g, cn_b,
           w_attn_proj, w_conv_proj, w_out):
    batch, seq, d = x.shape
    depth = w_in.shape[0]
    assert d == D_MODEL and seq % IN_TILE_M == 0 and seq % TAIL_TILE_M == 0
    cos_t, sin_t = _rope_tables(seq)
    bf = jnp.bfloat16
    w_qk, w_rest = _prepare_w_in(w_in)
    xf = x.reshape(batch * seq, d)
    for l in range(depth):
        p_arr, kmean, vt = _in_proj(
            l, xf, norm_g[l][None], w_qk, w_rest, b_gate[l][None],
            _permute_head_dims(q_norm_g[l])[None], _permute_head_dims(k_norm_g[l])[None],
            cos_t, sin_t, seq)
        kmean = kmean.reshape(batch, seq // MOBA_BLOCK, D_MODEL)
        score_bound = (LOG2_E * HEAD_DIM ** 0.5 * jnp.max(jnp.abs(q_norm_g[l]))
                       * jnp.max(jnp.abs(k_norm_g[l]))).reshape(1)
        o = _attention(score_bound, p_arr, vt, kmean, batch, seq)
        xf = _tail(p_arr, o, xf, w_attn_proj[l].astype(bf), w_conv_proj[l].astype(bf),
                   w_out[l].astype(bf), conv_w[l], conv_b[l][None], cn_g[l][None],
                   cn_b[l][None], batch, seq)
    return xf.reshape(batch, seq, d)
```
